```python
import jax, jax.numpy as jnp
from jax import lax
import numpy as np

D_MODEL = 4096
BATCH = 4
SEQ = 2048
DEPTH = 1
DEC_BATCH = 16
DEC_SEQ = 32
PAST_LEN = 1024

CHUNK = 64
RMS_EPS = 1e-6
D_A = D_MODEL // 2
POOL_WINDOWS = (2, 4, 8, 16)
N_POOL_GROUPS = 4
POOL_GROUP = D_A // N_POOL_GROUPS
POOL_BUF = 15
DK_B = 128
DV_B = 128
H_B = D_MODEL // DK_B
HK = H_B * DK_B
D_B = H_B * DV_B
HGRN_BLOCK = 16
IN_WIDTH = 2 * D_A + 2 * HK + 2 * D_B + 2 * D_MODEL

kernel_name = "pool_hgrn2_gated_parallel_stream_step"


def rms_norm(x, w):
    xf = x.astype(jnp.float32)
    y = xf * lax.rsqrt(jnp.mean(xf * xf, axis=-1, keepdims=True) + RMS_EPS)
    return (y * w.astype(jnp.float32)).astype(x.dtype)


def pool_mix(u, buf, start_pos, w_pool, pool_scale):
    B, T, _ = u.shape
    ext = jnp.concatenate([buf.astype(u.dtype), u], axis=1)
    extf = ext.astype(jnp.float32)
    cs = jnp.concatenate([jnp.zeros_like(extf[:, :1]), jnp.cumsum(extf, axis=1)], axis=1)
    pos = start_pos + jnp.arange(T)
    means = []
    for g, w in enumerate(POOL_WINDOWS):
        ch = slice(g * POOL_GROUP, (g + 1) * POOL_GROUP)
        s = cs[:, POOL_BUF + 1:POOL_BUF + 1 + T, ch] - cs[:, POOL_BUF + 1 - w:POOL_BUF + 1 - w + T, ch]
        cnt = jnp.minimum(w, pos + 1).astype(jnp.float32)
        means.append(s / cnt[None, :, None])
    pooled = jnp.concatenate(means, axis=-1) - extf[:, POOL_BUF:]
    mixed = jnp.einsum('btgc,gcd->btgd', pooled.reshape(B, T, N_POOL_GROUPS, POOL_GROUP),
                       w_pool.astype(jnp.float32)).reshape(B, T, D_A)
    out = mixed * pool_scale.astype(jnp.float32)
    new_buf = ext[:, -POOL_BUF:]
    return out, new_buf


def hgrn2_recurrence(q, k, v, log_f, S0):
    B, T = q.shape[:2]
    L = HGRN_BLOCK
    n_blk = -(-T // L)
    pad = n_blk * L - T

    def prep(a):
        a = jnp.pad(a, ((0, 0), (0, pad), (0, 0), (0, 0)))
        return a.reshape(B, n_blk, L, a.shape[2], a.shape[3]).transpose(1, 0, 3, 2, 4)

    mask = jnp.tril(jnp.ones((L, L), dtype=bool))

    def step(S, blk):
        qb, kb, vb, gb = blk
        b = jnp.cumsum(gb, axis=2)
        o_inter = jnp.einsum('bhtk,bhkv->bhtv', qb * jnp.exp(b), S)
        diff = b[:, :, :, None, :] - b[:, :, None, :, :]
        decay = jnp.exp(jnp.where(mask[:, :, None], diff, -jnp.inf))
        scores = jnp.einsum('bhtk,bhtsk,bhsk->bhts', qb, decay, kb)
        o = o_inter + jnp.einsum('bhts,bhsv->bhtv', scores, vb)
        b_last = b[:, :, -1:, :]
        S_new = jnp.exp(b_last[:, :, 0, :, None]) * S + jnp.einsum('bhsk,bhsv->bhkv', kb * jnp.exp(b_last - b), vb)
        return S_new, o

    S_fin, o = lax.scan(step, S0, (prep(q), prep(k), prep(v), prep(log_f)))
    o = o.transpose(1, 0, 3, 2, 4).reshape(B, n_blk * L, H_B, DV_B)[:, :T]
    return o, S_fin


def hgrn2_branch(q, f, i_, z, S0, lb, g_norm):
    B, T = q.shape[:2]
    qh = (jax.nn.silu(q.astype(jnp.float32)) * (DK_B ** -0.5)).reshape(B, T, H_B, DK_B)
    forget = lb + (1.0 - lb) * jax.nn.sigmoid(f.astype(jnp.float32))
    kh = (1.0 - forget).reshape(B, T, H_B, DK_B)
    log_f = jnp.log(forget).reshape(B, T, H_B, DK_B)
    vh = i_.astype(jnp.float32).reshape(B, T, H_B, DV_B)
    o, S_fin = hgrn2_recurrence(qh, kh, log_f=log_f, v=vh, S0=S0.astype(jnp.float32))
    o = o * lax.rsqrt(jnp.mean(o * o, axis=-1, keepdims=True) + RMS_EPS) * g_norm.astype(jnp.float32)
    o = o.reshape(B, T, D_B) * jax.nn.silu(z.astype(jnp.float32))
    return o, S_fin


def layer(x, pool_buf, S0, start_pos, lb, norm_pre, norm_post, w_in, w_pool, pool_scale,
          g_norm, w_branch_a, w_branch_b, b_gate, w_out):
    xn = rms_norm(x, norm_pre)
    proj = jnp.einsum('btd,de->bte', xn, w_in)
    u_a, z_a, q, f, i_, z_b, g_a, g_b = jnp.split(
        proj, [D_A, 2 * D_A, 2 * D_A + HK, 2 * D_A + 2 * HK, 2 * D_A + 2 * HK + D_B,
               2 * D_A + 2 * HK + 2 * D_B, 2 * D_A + 2 * HK + 2 * D_B + D_MODEL], axis=-1)
    pa, new_buf = pool_mix(u_a, pool_buf, start_pos, w_pool, pool_scale)
    ya = jnp.einsum('bte,ed->btd', (pa * jax.nn.silu(z_a.astype(jnp.float32))).astype(x.dtype), w_branch_a)
    ob, S_fin = hgrn2_branch(q, f, i_, z_b, S0, lb, g_norm)
    yb = jnp.einsum('bte,ed->btd', ob.astype(x.dtype), w_branch_b)
    merged = jax.nn.sigmoid(g_a + b_gate[0]) * ya + jax.nn.sigmoid(g_b + b_gate[1]) * yb
    out = jnp.einsum('btd,de->bte', merged, w_out)
    y = x + rms_norm(out, norm_post)
    return y, new_buf, S_fin.astype(x.dtype)


def setup_inputs(seed: int = 0) -> dict:
    key = jax.random.key(seed)
    ks = jax.random.split(key, 16)
    f32 = jnp.float32
    n = jax.random.normal
    return {
        "x_prompt": n(ks[0], (BATCH, SEQ, D_MODEL), f32),
        "x_sample": n(ks[1], (DEC_BATCH, DEC_SEQ, D_MODEL), f32),
        "state_pool": n(ks[2], (DEPTH, DEC_BATCH, POOL_BUF, D_A), f32),
        "state_hgrn": 0.5 * n(ks[3], (DEPTH, DEC_BATCH, H_B, DK_B, DV_B), f32),
        "norm_pre": 1.0 + 0.02 * n(ks[4], (DEPTH, D_MODEL), f32),
        "norm_post": 1.0 + 0.02 * n(ks[5], (DEPTH, D_MODEL), f32),
        "w_in": n(ks[6], (DEPTH, D_MODEL, IN_WIDTH), f32) * D_MODEL ** -0.5,
        "w_pool": n(ks[7], (DEPTH, N_POOL_GROUPS, POOL_GROUP, POOL_GROUP), f32) * POOL_GROUP ** -0.5,
        "pool_scale": 1.0 + 0.02 * n(ks[8], (DEPTH, D_A), f32),
        "lb_logits": 0.5 * n(ks[9], (DEPTH + 1, HK), f32),
        "g_norm": 1.0 + 0.02 * n(ks[10], (DEPTH, DV_B), f32),
        "w_branch_a": n(ks[11], (DEPTH, D_A, D_MODEL), f32) * D_A ** -0.5,
        "w_branch_b": n(ks[12], (DEPTH, D_B, D_MODEL), f32) * D_B ** -0.5,
        "b_gate": 0.02 * n(ks[13], (DEPTH, 2, D_MODEL), f32),
        "w_out": n(ks[14], (DEPTH, D_MODEL, D_MODEL), f32) * D_MODEL ** -0.5,
    }


def reference(x_prompt, x_sample, state_pool, state_hgrn, norm_pre, norm_post, w_in, w_pool,
              pool_scale, lb_logits, g_norm, w_branch_a, w_branch_b, b_gate, w_out):
    lb_all = jnp.cumsum(jax.nn.softmax(lb_logits.astype(jnp.float32), axis=0), axis=0)
    hp, hs = x_prompt, x_sample
    pool_p, hgrn_p, pool_s, hgrn_s = [], [], [], []
    for l in range(DEPTH):
        params = (norm_pre[l], norm_post[l], w_in[l], w_pool[l], pool_scale[l], g_norm[l],
                  w_branch_a[l], w_branch_b[l], b_gate[l], w_out[l])
        buf0 = jnp.zeros((hp.shape[0], POOL_BUF, D_A), hp.dtype)
        S0 = jnp.zeros((hp.shape[0], H_B, DK_B, DV_B), jnp.float32)
        hp, bp, sp = layer(hp, buf0, S0, 0, lb_all[l], *params)
        hs, bs, ss = layer(hs, state_pool[l], state_hgrn[l], PAST_LEN, lb_all[l], *params)
        pool_p.append(bp)
        hgrn_p.append(sp)
        pool_s.append(bs)
        hgrn_s.append(ss)
    return (hp, hs, jnp.stack(pool_p), jnp.stack(hgrn_p), jnp.stack(pool_s), jnp.stack(hgrn_s))
```

```python
import functools

import jax
import jax.numpy as jnp
from jax import lax
from jax.experimental import pallas as pl
from jax.experimental.pallas import tpu as pltpu

F32 = jnp.float32
BF16 = jnp.bfloat16

RMS_EPS = 1e-6
POOL_WINDOWS = (2, 4, 8, 16)
POOL_BUF = 15
POOL_HALO = 16
HEAD_DIM = 128
V7X_VMEM_LIMIT_BYTES = 56 * 1024 * 1024


def _params(n_axes):
    return pltpu.CompilerParams(
        dimension_semantics=("arbitrary",) * n_axes,
        vmem_limit_bytes=V7X_VMEM_LIMIT_BYTES,
    )


def _sigmoid(x):
    return 1.0 / (1.0 + jnp.exp(-x))


def _silu(x):
    return x * _sigmoid(x)


def _dot(a, b):
    return jnp.dot(a, b, preferred_element_type=F32)


def _dot_nt(a, b):
    return lax.dot_general(a, b, (((1,), (1,)), ((), ())), preferred_element_type=F32)


def _dot_tn(a, b):
    return lax.dot_general(a, b, (((0,), (0,)), ((), ())), preferred_element_type=F32)


def _inproj_kernel(x_ref, nw_ref, w_ref, o_ref, xn_ref):
    @pl.when(pl.program_id(1) == 0)
    def _():
        x = x_ref[...]
        ms = jnp.mean(x * x, axis=-1, keepdims=True)
        xn_ref[...] = (x * lax.rsqrt(ms + RMS_EPS) * nw_ref[...]).astype(BF16)

    o_ref[...] = _dot(xn_ref[...], w_ref[...]).astype(o_ref.dtype)


def _inproj(x2d, norm_w, w, *, tm, tn):
    n, d = x2d.shape
    e = w.shape[1]
    return pl.pallas_call(
        _inproj_kernel,
        grid=(n // tm, e // tn),
        in_specs=[
            pl.BlockSpec((tm, d), lambda i, j: (i, 0)),
            pl.BlockSpec((1, d), lambda i, j: (0, 0)),
            pl.BlockSpec((d, tn), lambda i, j: (0, j)),
        ],
        out_specs=pl.BlockSpec((tm, tn), lambda i, j: (i, j)),
        out_shape=jax.ShapeDtypeStruct((n, e), BF16),
        scratch_shapes=[pltpu.VMEM((tm, d), BF16)],
        compiler_params=_params(2),
        name="inproj",
    )(x2d, norm_w, w)


def _pool_kernel(u_ref, z_ref, buf_ref, wp_ref, sc_ref, h_ref, nb_ref, prev_ref, *, tt, start_pos):
    t = pl.program_id(1)
    pg = wp_ref.shape[1]

    @pl.when(t == 0)
    def _():
        prev_ref[...] = buf_ref[...]

    u = u_ref[...].astype(F32)
    ext = jnp.concatenate([prev_ref[...], u], axis=0)
    pos = start_pos + t * tt + lax.broadcasted_iota(jnp.int32, (tt, 1), 0)
    for g, w in enumerate(POOL_WINDOWS):
        cs = slice(g * pg, (g + 1) * pg)
        cur = ext[:, cs]
        d = 1
        while d < w:
            cur = cur[d:] + cur[:-d]
            d *= 2
        first = POOL_HALO + 1 - w
        s = cur[first:first + tt]
        cnt = jnp.minimum(w, pos + 1).astype(F32)
        pooled = s / cnt - u[:, cs]
        mixed = _dot(pooled.astype(BF16), wp_ref[g])
        z = z_ref[:, cs].astype(F32)
        h_ref[:, cs] = (mixed * sc_ref[:, cs] * _silu(z)).astype(h_ref.dtype)
    prev_ref[...] = ext[tt:, :]

    @pl.when(t == pl.num_programs(1) - 1)
    def _():
        nb_ref[...] = ext[tt + POOL_HALO - POOL_BUF:, :]


def _pool_branch(proj, buf, w_pool, pool_scale, *, n_streams, seq, tt, start_pos):
    d_a = buf.shape[-1]
    nt = seq // tt
    kern = functools.partial(_pool_kernel, tt=tt, start_pos=start_pos)
    return pl.pallas_call(
        kern,
        grid=(n_streams, nt),
        in_specs=[
            pl.BlockSpec((tt, d_a), lambda b, t: (b * nt + t, 0)),
            pl.BlockSpec((tt, d_a), lambda b, t: (b * nt + t, 1)),
            pl.BlockSpec((None, POOL_HALO, d_a), lambda b, t: (b, 0, 0)),
            pl.BlockSpec(w_pool.shape, lambda b, t: (0, 0, 0)),
            pl.BlockSpec((1, d_a), lambda b, t: (0, 0)),
        ],
        out_specs=[
            pl.BlockSpec((tt, d_a), lambda b, t: (b * nt + t, 0)),
            pl.BlockSpec((None, POOL_BUF, d_a), lambda b, t: (b, 0, 0)),
        ],
        out_shape=[
            jax.ShapeDtypeStruct((n_streams * seq, d_a), BF16),
            jax.ShapeDtypeStruct((n_streams, POOL_BUF, d_a), F32),
        ],
        scratch_shapes=[pltpu.VMEM((POOL_HALO, d_a), F32)],
        compiler_params=_params(2),
        name="pool_branch",
    )(proj, proj, buf, w_pool, pool_scale)


def _gap_log_decay(b, lf, half):
    c = b.shape[0]
    t = lax.broadcasted_iota(jnp.int32, b.shape, 0)
    if half == 1:
        return jnp.where((t & 1) == 1, lf, 0.0)
    if half == 2:
        r = t & 3
        up = pltpu.roll(lf, 1, 0)
        dn = pltpu.roll(lf, c - 1, 0)
        return jnp.where(r == 0, dn, jnp.where(r == 1, 0.0, jnp.where(r == 2, lf, lf + up)))
    blk = 2 * half
    pieces = []
    for i in range(c // blk):
        m = i * blk + half - 1
        pieces.append(jnp.broadcast_to(b[m:m + 1, :], (blk, b.shape[1])))
    bm = pieces[0] if len(pieces) == 1 else jnp.concatenate(pieces, axis=0)
    return -jnp.abs(b - bm)


def _hgrn_kernel(*refs, hp, chunk, n_chunks, has_state, layer):
    if has_state:
        q_ref, f_ref, i_ref, z_ref, lbl_ref, gn_ref, s0_ref, o_ref, sfin_ref, st_ref = refs
    else:
        q_ref, f_ref, i_ref, z_ref, lbl_ref, gn_ref, o_ref, sfin_ref, st_ref = refs
    c = chunk
    n_levels = c.bit_length() - 1
    row = lax.broadcasted_iota(jnp.int32, (c, c), 0)
    col = lax.broadcasted_iota(jnp.int32, (c, c), 1)
    tri = (row >= col).astype(BF16)
    xr = row ^ col
    causal = row > col
    diag = row == col
    lg = lbl_ref[...]
    lmax = jnp.max(lg, axis=0, keepdims=True)
    le = jnp.exp(lg - lmax)
    lb_all = (jnp.sum(le[0:layer + 1, :], axis=0, keepdims=True)
              / jnp.sum(le, axis=0, keepdims=True))
    gn = gn_ref[...]

    for h in range(hp):
        ls = slice(h * HEAD_DIM, (h + 1) * HEAD_DIM)
        lb = lb_all[:, ls]
        if has_state:
            st_ref[h] = s0_ref[h].T
        else:
            st_ref[h] = jnp.zeros((HEAD_DIM, HEAD_DIM), F32)

        def body(ci, carry, ls=ls, lb=lb, h=h):
            rows = pl.ds(pl.multiple_of(ci * c, c), c)
            qr = q_ref[rows, ls].astype(F32)
            q = _silu(qr) * (HEAD_DIM ** -0.5)
            forget = lb + (1.0 - lb) * _sigmoid(f_ref[rows, ls].astype(F32))
            k = 1.0 - forget
            lf = jnp.log(forget)
            v = i_ref[rows, ls]
            hi = lf.astype(BF16)
            r1 = lf - hi.astype(F32)
            mid = r1.astype(BF16)
            lo = (r1 - mid.astype(F32)).astype(BF16)
            b = _dot(tri, hi) + _dot(tri, mid) + _dot(tri, lo)
            b_last = b[c - 1:c, :]
            st = st_ref[h]
            o = _dot_nt((q * jnp.exp(b)).astype(BF16), st.astype(BF16))
            a = jnp.where(diag, _dot_nt(q.astype(BF16), k.astype(BF16)), 0.0)
            for lv in range(n_levels):
                e = jnp.exp(_gap_log_decay(b, lf, 1 << lv))
                a_lv = _dot_nt((q * e).astype(BF16), (k * e).astype(BF16))
                a = jnp.where(causal & ((xr >> lv) == 1), a_lv, a)
            o = o + _dot(a.astype(BF16), v)
            k_end = (k * jnp.exp(b_last - b)).astype(BF16)
            st_ref[h] = st * jnp.exp(b_last) + _dot_tn(v, k_end)
            o = o * lax.rsqrt(jnp.mean(o * o, axis=-1, keepdims=True) + RMS_EPS) * gn
            o_ref[rows, ls] = (o * _silu(z_ref[rows, ls].astype(F32))).astype(o_ref.dtype)
            return carry

        lax.fori_loop(0, n_chunks, body, 0)
        sfin_ref[h] = st_ref[h].T


def _hgrn_branch(proj, lb_logits, g_norm, state, *, n_streams, seq, hp, chunk, col0, layer):
    hk = lb_logits.shape[1]
    n_heads = hk // HEAD_DIM
    wblk = hp * HEAD_DIM
    seg = hk // wblk
    c0 = col0 // wblk
    has_state = state is not None
    kern = functools.partial(_hgrn_kernel, hp=hp, chunk=chunk, n_chunks=seq // chunk, has_state=has_state,
                             layer=layer)

    def seg_spec(k):
        return pl.BlockSpec((seq, wblk), lambda b, j: (b, c0 + k * seg + j))

    in_specs = [seg_spec(0), seg_spec(1), seg_spec(2), seg_spec(3),
                pl.BlockSpec((lb_logits.shape[0], wblk), lambda b, j: (0, j)),
                pl.BlockSpec((1, HEAD_DIM), lambda b, j: (0, 0))]
    args = [proj, proj, proj, proj, lb_logits, g_norm]
    state_spec = pl.BlockSpec((None, hp, HEAD_DIM, HEAD_DIM), lambda b, j: (b, j, 0, 0))
    if has_state:
        in_specs.append(state_spec)
        args.append(state)
    return pl.pallas_call(
        kern,
        grid=(n_streams, n_heads // hp),
        in_specs=in_specs,
        out_specs=[pl.BlockSpec((seq, wblk), lambda b, j: (b, j)), state_spec],
        out_shape=[
            jax.ShapeDtypeStruct((n_streams * seq, hk), BF16),
            jax.ShapeDtypeStruct((n_streams, n_heads, HEAD_DIM, HEAD_DIM), F32),
        ],
        scratch_shapes=[pltpu.VMEM((hp, HEAD_DIM, HEAD_DIM), F32)],
        compiler_params=_params(2),
        name="hgrn_branch",
    )(*args)


def _merge_kernel(ha_ref, ob_ref, ga_ref, gb_ref, bg_ref, wa_ref, wb_ref, m_ref):
    ya = _dot(ha_ref[...], wa_ref[...])
    yb = _dot(ob_ref[...], wb_ref[...])
    ga = _sigmoid(ga_ref[...].astype(F32) + bg_ref[0:1, :])
    gb = _sigmoid(gb_ref[...].astype(F32) + bg_ref[1:2, :])
    m_ref[...] = (ga * ya + gb * yb).astype(m_ref.dtype)


def _merge(h_a, ob, proj, b_gate, w_a, w_b, *, tm, tn, gate_col0):
    n, d_a = h_a.shape
    d_b = ob.shape[1]
    d = w_a.shape[1]
    ga0 = gate_col0 // tn
    gb0 = (gate_col0 + d) // tn
    return pl.pallas_call(
        _merge_kernel,
        grid=(n // tm, d // tn),
        in_specs=[
            pl.BlockSpec((tm, d_a), lambda i, j: (i, 0)),
            pl.BlockSpec((tm, d_b), lambda i, j: (i, 0)),
            pl.BlockSpec((tm, tn), lambda i, j: (i, ga0 + j)),
            pl.BlockSpec((tm, tn), lambda i, j: (i, gb0 + j)),
            pl.BlockSpec((2, tn), lambda i, j: (0, j)),
            pl.BlockSpec((d_a, tn), lambda i, j: (0, j)),
            pl.BlockSpec((d_b, tn), lambda i, j: (0, j)),
        ],
        out_specs=pl.BlockSpec((tm, tn), lambda i, j: (i, j)),
        out_shape=jax.ShapeDtypeStruct((n, d), BF16),
        compiler_params=_params(2),
        name="merge",
    )(h_a, ob, proj, proj, b_gate, w_a, w_b)


def _outproj_kernel(m_ref, x_ref, nw_ref, w_ref, y_ref, acc_ref, *, tn):
    j = pl.program_id(1)
    cols = pl.ds(pl.multiple_of(j * tn, tn), tn)
    acc_ref[:, cols] = _dot(m_ref[...], w_ref[...])

    @pl.when(j == pl.num_programs(1) - 1)
    def _():
        out = acc_ref[...]
        ms = jnp.mean(out * out, axis=-1, keepdims=True)
        y_ref[...] = x_ref[...] + out * lax.rsqrt(ms + RMS_EPS) * nw_ref[...]


def _outproj(m, x2d, norm_w, w, *, tm, tn):
    n, d = x2d.shape
    return pl.pallas_call(
        functools.partial(_outproj_kernel, tn=tn),
        grid=(n // tm, d // tn),
        in_specs=[
            pl.BlockSpec((tm, d), lambda i, j: (i, 0)),
            pl.BlockSpec((tm, d), lambda i, j: (i, 0)),
            pl.BlockSpec((1, d), lambda i, j: (0, 0)),
            pl.BlockSpec((d, tn), lambda i, j: (0, j)),
        ],
        out_specs=pl.BlockSpec((tm, d), lambda i, j: (i, 0)),
        out_shape=jax.ShapeDtypeStruct((n, d), F32),
        scratch_shapes=[pltpu.VMEM((tm, d), F32)],
        compiler_params=_params(2),
        name="outproj",
    )(m, x2d, norm_w, w)


def _layer(x, pool_buf, state, start_pos, p, layer, *, hgrn_hp, hgrn_chunk, pool_tt, tm_in, tm_out):
    n_streams, seq, d = x.shape
    d_a = p["w_pool"].shape[0] * p["w_pool"].shape[1]
    hk = p["lb_logits"].shape[1]
    x2d = x.reshape(n_streams * seq, d)
    proj = _inproj(x2d, p["norm_pre"], p["w_in"], tm=tm_in, tn=1024)
    buf = jnp.pad(pool_buf.astype(F32), ((0, 0), (POOL_HALO - POOL_BUF, 0), (0, 0)))
    h_a, new_buf = _pool_branch(proj, buf, p["w_pool"], p["pool_scale"],
                                n_streams=n_streams, seq=seq, tt=pool_tt, start_pos=start_pos)
    ob, s_fin = _hgrn_branch(proj, p["lb_logits"], p["g_norm"], state,
                             n_streams=n_streams, seq=seq, hp=hgrn_hp, chunk=hgrn_chunk, col0=2 * d_a,
                             layer=layer)
    m = _merge(h_a, ob, proj, p["b_gate"], p["w_branch_a"], p["w_branch_b"],
               tm=tm_in, tn=1024, gate_col0=2 * d_a + 4 * hk)
    y = _outproj(m, x2d, p["norm_post"], p["w_out"], tm=tm_out, tn=1024)
    return y.reshape(x.shape), new_buf, s_fin


def kernel(x_prompt, x_sample, state_pool, state_hgrn, norm_pre, norm_post, w_in, w_pool, pool_scale,
           lb_logits, g_norm, w_branch_a, w_branch_b, b_gate, w_out):
    depth = w_in.shape[0]
    past_len = 1024
    hp, hs = x_prompt, x_sample
    pool_p, hgrn_p, pool_s, hgrn_s = [], [], [], []
    for l in range(depth):
        p = {
            "norm_pre": norm_pre[l][None, :],
            "norm_post": norm_post[l][None, :],
            "w_in": w_in[l].astype(BF16),
            "w_pool": w_pool[l].astype(BF16),
            "pool_scale": pool_scale[l][None, :],
            "lb_logits": lb_logits,
            "g_norm": g_norm[l][None, :],
            "w_branch_a": w_branch_a[l].astype(BF16),
            "w_branch_b": w_branch_b[l].astype(BF16),
            "b_gate": b_gate[l],
            "w_out": w_out[l].astype(BF16),
        }
        buf0 = jnp.zeros((hp.shape[0], POOL_BUF, state_pool.shape[-1]), F32)
        hp, bp, sp = _layer(hp, buf0, None, 0, p, l,
                            hgrn_hp=2, hgrn_chunk=128, pool_tt=512, tm_in=512, tm_out=256)
        hs, bs, ss = _layer(hs, state_pool[l], state_hgrn[l], past_len, p, l,
                            hgrn_hp=8, hgrn_chunk=32, pool_tt=32, tm_in=512, tm_out=256)
        pool_p.append(bp)
        hgrn_p.append(sp)
        pool_s.append(bs)
        hgrn_s.append(ss)
    return (hp, hs, jnp.stack(pool_p), jnp.stack(hgrn_p), jnp.stack(pool_s), jnp.stack(hgrn_s))
```

```python
import functools

import jax
import jax.numpy as jnp
from jax import lax
from jax.experimental import pallas as pl
from jax.experimental.pallas import tpu as pltpu

F32 = jnp.float32
BF16 = jnp.bfloat16

RMS_EPS = 1e-6
POOL_WINDOWS = (2, 4, 8, 16)
POOL_BUF = 15
POOL_HALO = 16
HEAD_DIM = 128
LOG2_E = 1.4426950408889634
V7X_VMEM_LIMIT_BYTES = 56 * 1024 * 1024


def _params(n_axes):
    return pltpu.CompilerParams(
        dimension_semantics=("arbitrary",) * n_axes,
        vmem_limit_bytes=V7X_VMEM_LIMIT_BYTES,
    )


def _sigmoid(x):
    return 1.0 / (1.0 + jnp.exp(-x))


def _silu(x):
    return x * _sigmoid(x)


def _dot(a, b):
    return jnp.dot(a, b, preferred_element_type=F32)


def _dot_nt(a, b):
    return lax.dot_general(a, b, (((1,), (1,)), ((), ())), preferred_element_type=F32)


def _dot_tn(a, b):
    return lax.dot_general(a, b, (((0,), (0,)), ((), ())), preferred_element_type=F32)


def _inproj_kernel(x_ref, nw_ref, w_ref, o_ref, xn_ref):
    @pl.when(pl.program_id(1) == 0)
    def _():
        x = x_ref[...]
        ms = jnp.mean(x * x, axis=-1, keepdims=True)
        xn_ref[...] = (x * lax.rsqrt(ms + RMS_EPS) * nw_ref[...]).astype(BF16)

    o_ref[...] = _dot(xn_ref[...], w_ref[...]).astype(o_ref.dtype)


def _inproj(x2d, norm_w, w, *, tm, tn):
    n, d = x2d.shape
    e = w.shape[1]
    return pl.pallas_call(
        _inproj_kernel,
        grid=(n // tm, e // tn),
        in_specs=[
            pl.BlockSpec((tm, d), lambda i, j: (i, 0)),
            pl.BlockSpec((1, d), lambda i, j: (0, 0)),
            pl.BlockSpec((d, tn), lambda i, j: (0, j)),
        ],
        out_specs=pl.BlockSpec((tm, tn), lambda i, j: (i, j)),
        out_shape=jax.ShapeDtypeStruct((n, e), BF16),
        scratch_shapes=[pltpu.VMEM((tm, d), BF16)],
        compiler_params=_params(2),
        name="inproj",
    )(x2d, norm_w, w)


def _pool_kernel(u_ref, z_ref, buf_ref, wp_ref, sc_ref, h_ref, nb_ref, prev_ref, *, tt, start_pos):
    t = pl.program_id(1)
    pg = wp_ref.shape[1]

    @pl.when(t == 0)
    def _():
        prev_ref[...] = buf_ref[...]

    u = u_ref[...].astype(F32)
    ext = jnp.concatenate([prev_ref[...], u], axis=0)
    pos = start_pos + t * tt + lax.broadcasted_iota(jnp.int32, (tt, 1), 0)
    for g, w in enumerate(POOL_WINDOWS):
        cs = slice(g * pg, (g + 1) * pg)
        cur = ext[:, cs]
        d = 1
        while d < w:
            cur = cur[d:] + cur[:-d]
            d *= 2
        first = POOL_HALO + 1 - w
        s = cur[first:first + tt]
        cnt = jnp.minimum(w, pos + 1).astype(F32)
        pooled = s / cnt - u[:, cs]
        mixed = _dot(pooled.astype(BF16), wp_ref[g])
        z = z_ref[:, cs].astype(F32)
        h_ref[:, cs] = (mixed * sc_ref[:, cs] * _silu(z)).astype(h_ref.dtype)
    prev_ref[...] = ext[tt:, :]

    @pl.when(t == pl.num_programs(1) - 1)
    def _():
        nb_ref[...] = ext[tt + POOL_HALO - POOL_BUF:, :]


def _pool_branch(proj, buf, w_pool, pool_scale, *, n_streams, seq, tt, start_pos):
    d_a = buf.shape[-1]
    nt = seq // tt
    kern = functools.partial(_pool_kernel, tt=tt, start_pos=start_pos)
    return pl.pallas_call(
        kern,
        grid=(n_streams, nt),
        in_specs=[
            pl.BlockSpec((tt, d_a), lambda b, t: (b * nt + t, 0)),
            pl.BlockSpec((tt, d_a), lambda b, t: (b * nt + t, 1)),
            pl.BlockSpec((None, POOL_HALO, d_a), lambda b, t: (b, 0, 0)),
            pl.BlockSpec(w_pool.shape, lambda b, t: (0, 0, 0)),
            pl.BlockSpec((1, d_a), lambda b, t: (0, 0)),
        ],
        out_specs=[
            pl.BlockSpec((tt, d_a), lambda b, t: (b * nt + t, 0)),
            pl.BlockSpec((None, POOL_BUF, d_a), lambda b, t: (b, 0, 0)),
        ],
        out_shape=[
            jax.ShapeDtypeStruct((n_streams * seq, d_a), BF16),
            jax.ShapeDtypeStruct((n_streams, POOL_BUF, d_a), F32),
        ],
        scratch_shapes=[pltpu.VMEM((POOL_HALO, d_a), F32)],
        compiler_params=_params(2),
        name="pool_branch",
    )(proj, proj, buf, w_pool, pool_scale)


def _level_operands(q, k, b, lf, half):
    c = b.shape[0]
    if half >= 8:
        blk = 2 * half
        zeros = jnp.zeros((half, b.shape[1]), BF16)
        qs, ks = [], []
        for i in range(c // blk):
            lo = slice(i * blk, i * blk + half)
            up = slice(i * blk + half, (i + 1) * blk)
            bm = b[i * blk + half - 1:i * blk + half, :]
            qs += [zeros, (q[up] * jnp.exp2(b[up] - bm)).astype(BF16)]
            ks += [(k[lo] * jnp.exp2(bm - b[lo])).astype(BF16), zeros]
        return jnp.concatenate(qs, axis=0), jnp.concatenate(ks, axis=0)
    t = lax.broadcasted_iota(jnp.int32, b.shape, 0)
    if half == 1:
        gap = jnp.where((t & 1) == 1, lf, 0.0)
    elif half == 2:
        r = t & 3
        up = pltpu.roll(lf, 1, 0)
        dn = pltpu.roll(lf, c - 1, 0)
        gap = jnp.where(r == 0, dn, jnp.where(r == 1, 0.0, jnp.where(r == 2, lf, lf + up)))
    else:
        blk = 2 * half
        bm = jnp.concatenate(
            [jnp.broadcast_to(b[i * blk + half - 1:i * blk + half, :], (blk, b.shape[1])) for i in range(c // blk)],
            axis=0)
        gap = -jnp.abs(b - bm)
    e = jnp.exp2(gap)
    return (q * e).astype(BF16), (k * e).astype(BF16)


def _hgrn_chunk(q_raw, f_raw, v, z_raw, st, lb, gn, tri, lvl):
    c = q_raw.shape[0]
    n_levels = c.bit_length() - 1
    q = _silu(q_raw) * (HEAD_DIM ** -0.5)
    forget = lb + (1.0 - lb) * _sigmoid(f_raw)
    k = 1.0 - forget
    lf = jnp.log(forget) * LOG2_E
    hi = lf.astype(BF16)
    lo = (lf - hi.astype(F32)).astype(BF16)
    b = _dot(tri, hi) + _dot(tri, lo)
    b_last = b[c - 1:c, :]
    o = _dot_nt((q * jnp.exp2(b)).astype(BF16), st.astype(BF16))
    a = jnp.where(lvl == n_levels, _dot_nt(q.astype(BF16), k.astype(BF16)), 0.0)
    for lv in range(n_levels):
        ql, kl = _level_operands(q, k, b, lf, 1 << lv)
        a = jnp.where(lvl == lv, _dot_nt(ql, kl), a)
    o = o + _dot(a.astype(BF16), v)
    k_end = (k * jnp.exp2(b_last - b)).astype(BF16)
    st_new = st * jnp.exp2(b_last) + _dot_tn(v, k_end)
    o = o * lax.rsqrt(jnp.mean(o * o, axis=-1, keepdims=True) + RMS_EPS) * gn
    return o * _silu(z_raw), st_new


def _hgrn_kernel(*refs, hp, chunk, n_chunks, has_state, layer):
    if has_state:
        q_ref, f_ref, i_ref, z_ref, lbl_ref, gn_ref, s0_ref, o_ref, sfin_ref, st_ref, tri_ref, lvl_ref = refs
    else:
        q_ref, f_ref, i_ref, z_ref, lbl_ref, gn_ref, o_ref, sfin_ref, st_ref, tri_ref, lvl_ref = refs
    c = chunk
    n_levels = c.bit_length() - 1
    row = lax.broadcasted_iota(jnp.int32, (c, c), 0)
    col = lax.broadcasted_iota(jnp.int32, (c, c), 1)
    tri_ref[...] = (row >= col).astype(BF16)
    lvl_ref[...] = jnp.where(row > col, 31 - lax.clz(row ^ col), jnp.where(row == col, n_levels, -1))
    lg = lbl_ref[...]
    lmax = jnp.max(lg, axis=0, keepdims=True)
    le = jnp.exp(lg - lmax)
    lb_all = (jnp.sum(le[0:layer + 1, :], axis=0, keepdims=True)
              / jnp.sum(le, axis=0, keepdims=True))
    gn = gn_ref[...]

    for h in range(hp):
        if has_state:
            st_ref[h] = s0_ref[h].T
        else:
            st_ref[h] = jnp.zeros((HEAD_DIM, HEAD_DIM), F32)

    def body(ci, carry):
        rows = pl.ds(pl.multiple_of(ci * c, c), c)
        for h in range(hp):
            ls = slice(h * HEAD_DIM, (h + 1) * HEAD_DIM)
            o, st_new = _hgrn_chunk(q_ref[rows, ls].astype(F32), f_ref[rows, ls].astype(F32), i_ref[rows, ls],
                                    z_ref[rows, ls].astype(F32), st_ref[h], lb_all[:, ls], gn,
                                    tri_ref[...], lvl_ref[...])
            st_ref[h] = st_new
            o_ref[rows, ls] = o.astype(o_ref.dtype)
        return carry

    lax.fori_loop(0, n_chunks, body, 0, unroll=min(2, n_chunks))
    for h in range(hp):
        sfin_ref[h] = st_ref[h].T


def _hgrn_branch(proj, lb_logits, g_norm, state, *, n_streams, seq, hp, chunk, col0, layer):
    hk = lb_logits.shape[1]
    n_heads = hk // HEAD_DIM
    wblk = hp * HEAD_DIM
    seg = hk // wblk
    c0 = col0 // wblk
    has_state = state is not None
    kern = functools.partial(_hgrn_kernel, hp=hp, chunk=chunk, n_chunks=seq // chunk, has_state=has_state,
                             layer=layer)

    def seg_spec(k):
        return pl.BlockSpec((seq, wblk), lambda b, j: (b, c0 + k * seg + j))

    in_specs = [seg_spec(0), seg_spec(1), seg_spec(2), seg_spec(3),
                pl.BlockSpec((lb_logits.shape[0], wblk), lambda b, j: (0, j)),
                pl.BlockSpec((1, HEAD_DIM), lambda b, j: (0, 0))]
    args = [proj, proj, proj, proj, lb_logits, g_norm]
    state_spec = pl.BlockSpec((None, hp, HEAD_DIM, HEAD_DIM), lambda b, j: (b, j, 0, 0))
    if has_state:
        in_specs.append(state_spec)
        args.append(state)
    return pl.pallas_call(
        kern,
        grid=(n_streams, n_heads // hp),
        in_specs=in_specs,
        out_specs=[pl.BlockSpec((seq, wblk), lambda b, j: (b, j)), state_spec],
        out_shape=[
            jax.ShapeDtypeStruct((n_streams * seq, hk), BF16),
            jax.ShapeDtypeStruct((n_streams, n_heads, HEAD_DIM, HEAD_DIM), F32),
        ],
        scratch_shapes=[pltpu.VMEM((hp, HEAD_DIM, HEAD_DIM), F32),
                        pltpu.VMEM((chunk, chunk), BF16),
                        pltpu.VMEM((chunk, chunk), jnp.int32)],
        compiler_params=_params(2),
        name="hgrn_branch",
    )(*args)


def _merge_kernel(ha_ref, ob_ref, ga_ref, gb_ref, bg_ref, wa_ref, wb_ref, m_ref):
    ya = _dot(ha_ref[...], wa_ref[...])
    yb = _dot(ob_ref[...], wb_ref[...])
    ga = _sigmoid(ga_ref[...].astype(F32) + bg_ref[0:1, :])
    gb = _sigmoid(gb_ref[...].astype(F32) + bg_ref[1:2, :])
    m_ref[...] = (ga * ya + gb * yb).astype(m_ref.dtype)


def _merge(h_a, ob, proj, b_gate, w_a, w_b, *, tm, tn, gate_col0):
    n, d_a = h_a.shape
    d_b = ob.shape[1]
    d = w_a.shape[1]
    ga0 = gate_col0 // tn
    gb0 = (gate_col0 + d) // tn
    return pl.pallas_call(
        _merge_kernel,
        grid=(n // tm, d // tn),
        in_specs=[
            pl.BlockSpec((tm, d_a), lambda i, j: (i, 0)),
            pl.BlockSpec((tm, d_b), lambda i, j: (i, 0)),
            pl.BlockSpec((tm, tn), lambda i, j: (i, ga0 + j)),
            pl.BlockSpec((tm, tn), lambda i, j: (i, gb0 + j)),
            pl.BlockSpec((2, tn), lambda i, j: (0, j)),
            pl.BlockSpec((d_a, tn), lambda i, j: (0, j)),
            pl.BlockSpec((d_b, tn), lambda i, j: (0, j)),
        ],
        out_specs=pl.BlockSpec((tm, tn), lambda i, j: (i, j)),
        out_shape=jax.ShapeDtypeStruct((n, d), BF16),
        compiler_params=_params(2),
        name="merge",
    )(h_a, ob, proj, proj, b_gate, w_a, w_b)


def _outproj_kernel(m_ref, x_ref, nw_ref, w_ref, y_ref, acc_ref, *, tn):
    j = pl.program_id(1)
    cols = pl.ds(pl.multiple_of(j * tn, tn), tn)
    acc_ref[:, cols] = _dot(m_ref[...], w_ref[...])

    @pl.when(j == pl.num_programs(1) - 1)
    def _():
        out = acc_ref[...]
        ms = jnp.mean(out * out, axis=-1, keepdims=True)
        y_ref[...] = x_ref[...] + out * lax.rsqrt(ms + RMS_EPS) * nw_ref[...]


def _outproj(m, x2d, norm_w, w, *, tm, tn):
    n, d = x2d.shape
    return pl.pallas_call(
        functools.partial(_outproj_kernel, tn=tn),
        grid=(n // tm, d // tn),
        in_specs=[
            pl.BlockSpec((tm, d), lambda i, j: (i, 0)),
            pl.BlockSpec((tm, d), lambda i, j: (i, 0)),
            pl.BlockSpec((1, d), lambda i, j: (0, 0)),
            pl.BlockSpec((d, tn), lambda i, j: (0, j)),
        ],
        out_specs=pl.BlockSpec((tm, d), lambda i, j: (i, 0)),
        out_shape=jax.ShapeDtypeStruct((n, d), F32),
        scratch_shapes=[pltpu.VMEM((tm, d), F32)],
        compiler_params=_params(2),
        name="outproj",
    )(m, x2d, norm_w, w)


def _layer(x, pool_buf, state, start_pos, p, layer, *, hgrn_hp, hgrn_chunk, pool_tt, tm_in, tm_out):
    n_streams, seq, d = x.shape
    d_a = p["w_pool"].shape[0] * p["w_pool"].shape[1]
    hk = p["lb_logits"].shape[1]
    x2d = x.reshape(n_streams * seq, d)
    proj = _inproj(x2d, p["norm_pre"], p["w_in"], tm=tm_in, tn=1024)
    buf = jnp.pad(pool_buf.astype(F32), ((0, 0), (POOL_HALO - POOL_BUF, 0), (0, 0)))
    h_a, new_buf = _pool_branch(proj, buf, p["w_pool"], p["pool_scale"],
                                n_streams=n_streams, seq=seq, tt=pool_tt, start_pos=start_pos)
    ob, s_fin = _hgrn_branch(proj, p["lb_logits"], p["g_norm"], state,
                             n_streams=n_streams, seq=seq, hp=hgrn_hp, chunk=hgrn_chunk, col0=2 * d_a,
                             layer=layer)
    m = _merge(h_a, ob, proj, p["b_gate"], p["w_branch_a"], p["w_branch_b"],
               tm=tm_in, tn=1024, gate_col0=2 * d_a + 4 * hk)
    y = _outproj(m, x2d, p["norm_post"], p["w_out"], tm=tm_out, tn=1024)
    return y.reshape(x.shape), new_buf, s_fin


def kernel(x_prompt, x_sample, state_pool, state_hgrn, norm_pre, norm_post, w_in, w_pool, pool_scale,
           lb_logits, g_norm, w_branch_a, w_branch_b, b_gate, w_out):
    depth = w_in.shape[0]
    past_len = 1024
    hp, hs = x_prompt, x_sample
    pool_p, hgrn_p, pool_s, hgrn_s = [], [], [], []
    for l in range(depth):
        p = {
            "norm_pre": norm_pre[l][None, :],
            "norm_post": norm_post[l][None, :],
            "w_in": w_in[l].astype(BF16),
            "w_pool": w_pool[l].astype(BF16),
            "pool_scale": pool_scale[l][None, :],
            "lb_logits": lb_logits,
            "g_norm": g_norm[l][None, :],
            "w_branch_a": w_branch_a[l].astype(BF16),
            "w_branch_b": w_branch_b[l].astype(BF16),
            "b_gate": b_gate[l],
            "w_out": w_out[l].astype(BF16),
        }
        buf0 = jnp.zeros((hp.shape[0], POOL_BUF, state_pool.shape[-1]), F32)
        hp, bp, sp = _layer(hp, buf0, None, 0, p, l,
                            hgrn_hp=4, hgrn_chunk=128, pool_tt=512, tm_in=512, tm_out=256)
        hs, bs, ss = _layer(hs, state_pool[l], state_hgrn[l], past_len, p, l,
                            hgrn_hp=8, hgrn_chunk=32, pool_tt=32, tm_in=512, tm_out=256)
        pool_p.append(bp)
        hgrn_p.append(sp)
        pool_s.append(bs)
        hgrn_s.append(ss)
    return (hp, hs, jnp.stack(pool_p), jnp.stack(hgrn_p), jnp.stack(pool_s), jnp.stack(hgrn_s))
```

```python
import functools

import jax
import jax.numpy as jnp
from jax import lax
from jax.experimental import pallas as pl
from jax.experimental.pallas import tpu as pltpu

F32 = jnp.float32
BF16 = jnp.bfloat16

RMS_EPS = 1e-6
POOL_WINDOWS = (2, 4, 8, 16)
POOL_BUF = 15
POOL_HALO = 16
HEAD_DIM = 128
LOG2_E = 1.4426950408889634
V7X_VMEM_LIMIT_BYTES = 56 * 1024 * 1024


def _params(n_axes):
    return pltpu.CompilerParams(
        dimension_semantics=("arbitrary",) * n_axes,
        vmem_limit_bytes=V7X_VMEM_LIMIT_BYTES,
    )


def _sigmoid(x):
    return 1.0 / (1.0 + jnp.exp(-x))


def _silu(x):
    return x * _sigmoid(x)


def _dot(a, b):
    return jnp.dot(a, b, preferred_element_type=F32)


def _dot_nt(a, b):
    return lax.dot_general(a, b, (((1,), (1,)), ((), ())), preferred_element_type=F32)


def _dot_tn(a, b):
    return lax.dot_general(a, b, (((0,), (0,)), ((), ())), preferred_element_type=F32)


def _inproj_kernel(x_ref, nw_ref, w_ref, o_ref, xn_ref):
    @pl.when(pl.program_id(1) == 0)
    def _():
        x = x_ref[...]
        ms = jnp.mean(x * x, axis=-1, keepdims=True)
        xn_ref[...] = (x * lax.rsqrt(ms + RMS_EPS) * nw_ref[...]).astype(BF16)

    o_ref[...] = _dot(xn_ref[...], w_ref[...]).astype(o_ref.dtype)


def _inproj(x2d, norm_w, w, *, tm, tn):
    n, d = x2d.shape
    e = w.shape[1]
    return pl.pallas_call(
        _inproj_kernel,
        grid=(n // tm, e // tn),
        in_specs=[
            pl.BlockSpec((tm, d), lambda i, j: (i, 0)),
            pl.BlockSpec((1, d), lambda i, j: (0, 0)),
            pl.BlockSpec((d, tn), lambda i, j: (0, j)),
        ],
        out_specs=pl.BlockSpec((tm, tn), lambda i, j: (i, j)),
        out_shape=jax.ShapeDtypeStruct((n, e), BF16),
        scratch_shapes=[pltpu.VMEM((tm, d), BF16)],
        compiler_params=_params(2),
        name="inproj",
    )(x2d, norm_w, w)


def _pool_kernel(u_ref, z_ref, buf_ref, wp_ref, sc_ref, h_ref, nb_ref, prev_ref, *, tt, start_pos):
    t = pl.program_id(1)
    pg = wp_ref.shape[1]

    @pl.when(t == 0)
    def _():
        prev_ref[...] = buf_ref[...]

    u = u_ref[...].astype(F32)
    ext = jnp.concatenate([prev_ref[...], u], axis=0)
    pos = start_pos + t * tt + lax.broadcasted_iota(jnp.int32, (tt, 1), 0)
    for g, w in enumerate(POOL_WINDOWS):
        cs = slice(g * pg, (g + 1) * pg)
        cur = ext[:, cs]
        d = 1
        while d < w:
            cur = cur[d:] + cur[:-d]
            d *= 2
        first = POOL_HALO + 1 - w
        s = cur[first:first + tt]
        cnt = jnp.minimum(w, pos + 1).astype(F32)
        pooled = s / cnt - u[:, cs]
        mixed = _dot(pooled.astype(BF16), wp_ref[g])
        z = z_ref[:, cs].astype(F32)
        h_ref[:, cs] = (mixed * sc_ref[:, cs] * _silu(z)).astype(h_ref.dtype)
    prev_ref[...] = ext[tt:, :]

    @pl.when(t == pl.num_programs(1) - 1)
    def _():
        nb_ref[...] = ext[tt + POOL_HALO - POOL_BUF:, :]


def _pool_branch(proj, buf, w_pool, pool_scale, *, n_streams, seq, tt, start_pos):
    d_a = buf.shape[-1]
    nt = seq // tt
    kern = functools.partial(_pool_kernel, tt=tt, start_pos=start_pos)
    return pl.pallas_call(
        kern,
        grid=(n_streams, nt),
        in_specs=[
            pl.BlockSpec((tt, d_a), lambda b, t: (b * nt + t, 0)),
            pl.BlockSpec((tt, d_a), lambda b, t: (b * nt + t, 1)),
            pl.BlockSpec((None, POOL_HALO, d_a), lambda b, t: (b, 0, 0)),
            pl.BlockSpec(w_pool.shape, lambda b, t: (0, 0, 0)),
            pl.BlockSpec((1, d_a), lambda b, t: (0, 0)),
        ],
        out_specs=[
            pl.BlockSpec((tt, d_a), lambda b, t: (b * nt + t, 0)),
            pl.BlockSpec((None, POOL_BUF, d_a), lambda b, t: (b, 0, 0)),
        ],
        out_shape=[
            jax.ShapeDtypeStruct((n_streams * seq, d_a), BF16),
            jax.ShapeDtypeStruct((n_streams, POOL_BUF, d_a), F32),
        ],
        scratch_shapes=[pltpu.VMEM((POOL_HALO, d_a), F32)],
        compiler_params=_params(2),
        name="pool_branch",
    )(proj, proj, buf, w_pool, pool_scale)


def _level_operands(q, k, b, lf, half):
    c = b.shape[0]
    if half >= 8:
        blk = 2 * half
        zeros = jnp.zeros((half, b.shape[1]), BF16)
        qs, ks = [], []
        for i in range(c // blk):
            lo = slice(i * blk, i * blk + half)
            up = slice(i * blk + half, (i + 1) * blk)
            bm = b[i * blk + half - 1:i * blk + half, :]
            qs += [zeros, (q[up] * jnp.exp2(b[up] - bm)).astype(BF16)]
            ks += [(k[lo] * jnp.exp2(bm - b[lo])).astype(BF16), zeros]
        return jnp.concatenate(qs, axis=0), jnp.concatenate(ks, axis=0)
    t = lax.broadcasted_iota(jnp.int32, b.shape, 0)
    if half == 1:
        gap = jnp.where((t & 1) == 1, lf, 0.0)
    elif half == 2:
        r = t & 3
        up = pltpu.roll(lf, 1, 0)
        dn = pltpu.roll(lf, c - 1, 0)
        gap = jnp.where(r == 0, dn, jnp.where(r == 1, 0.0, jnp.where(r == 2, lf, lf + up)))
    else:
        blk = 2 * half
        bm = jnp.concatenate(
            [jnp.broadcast_to(b[i * blk + half - 1:i * blk + half, :], (blk, b.shape[1])) for i in range(c // blk)],
            axis=0)
        gap = -jnp.abs(b - bm)
    e = jnp.exp2(gap)
    return (q * e).astype(BF16), (k * e).astype(BF16)


def _hgrn_chunk(q_raw, f_raw, v, z_raw, st, lb, gn, tri, lvl):
    c = q_raw.shape[0]
    n_levels = c.bit_length() - 1
    q = _silu(q_raw) * (HEAD_DIM ** -0.5)
    forget = lb + (1.0 - lb) * _sigmoid(f_raw)
    k = 1.0 - forget
    lf = jnp.log(forget) * LOG2_E
    hi = lf.astype(BF16)
    lo = (lf - hi.astype(F32)).astype(BF16)
    b = _dot(tri, hi) + _dot(tri, lo)
    b_last = b[c - 1:c, :]
    o = _dot_nt((q * jnp.exp2(b)).astype(BF16), st.astype(BF16))
    a = jnp.where(lvl == n_levels, _dot_nt(q.astype(BF16), k.astype(BF16)), 0.0)
    for lv in range(n_levels):
        half = 1 << lv
        ql, kl = _level_operands(q, k, b, lf, half)
        a_lv = _dot_nt(ql, kl)
        if half < 8:
            a = jnp.where(lvl == lv, a_lv, a)
        else:
            pieces = []
            for i in range(c // (2 * half)):
                lo = slice(2 * i * half, (2 * i + 1) * half)
                up = slice((2 * i + 1) * half, (2 * i + 2) * half)
                pieces += [a[lo], jnp.where(lvl[up] == lv, a_lv[up], a[up])]
            a = jnp.concatenate(pieces, axis=0)
    o = o + _dot(a.astype(BF16), v)
    k_end = (k * jnp.exp2(b_last - b)).astype(BF16)
    st_new = st * jnp.exp2(b_last) + _dot_tn(v, k_end)
    o = o * lax.rsqrt(jnp.mean(o * o, axis=-1, keepdims=True) + RMS_EPS) * gn
    return o * _silu(z_raw), st_new


def _hgrn_kernel(*refs, hp, chunk, n_chunks, has_state, layer):
    if has_state:
        q_ref, f_ref, i_ref, z_ref, lbl_ref, gn_ref, s0_ref, o_ref, sfin_ref, st_ref, tri_ref, lvl_ref = refs
    else:
        q_ref, f_ref, i_ref, z_ref, lbl_ref, gn_ref, o_ref, sfin_ref, st_ref, tri_ref, lvl_ref = refs
    c = chunk
    n_levels = c.bit_length() - 1
    row = lax.broadcasted_iota(jnp.int32, (c, c), 0)
    col = lax.broadcasted_iota(jnp.int32, (c, c), 1)
    tri_ref[...] = (row >= col).astype(BF16)
    lvl_ref[...] = jnp.where(row > col, 31 - lax.clz(row ^ col), jnp.where(row == col, n_levels, -1))
    lg = lbl_ref[...]
    lmax = jnp.max(lg, axis=0, keepdims=True)
    le = jnp.exp(lg - lmax)
    lb_all = (jnp.sum(le[0:layer + 1, :], axis=0, keepdims=True)
              / jnp.sum(le, axis=0, keepdims=True))
    gn = gn_ref[...]

    for h in range(hp):
        if has_state:
            st_ref[h] = s0_ref[h].T
        else:
            st_ref[h] = jnp.zeros((HEAD_DIM, HEAD_DIM), F32)

    def body(ci, carry):
        rows = pl.ds(pl.multiple_of(ci * c, c), c)
        for h in range(hp):
            ls = slice(h * HEAD_DIM, (h + 1) * HEAD_DIM)
            o, st_new = _hgrn_chunk(q_ref[rows, ls].astype(F32), f_ref[rows, ls].astype(F32), i_ref[rows, ls],
                                    z_ref[rows, ls].astype(F32), st_ref[h], lb_all[:, ls], gn,
                                    tri_ref[...], lvl_ref[...])
            st_ref[h] = st_new
            o_ref[rows, ls] = o.astype(o_ref.dtype)
        return carry

    lax.fori_loop(0, n_chunks, body, 0, unroll=min(2, n_chunks))
    for h in range(hp):
        sfin_ref[h] = st_ref[h].T


def _hgrn_branch(proj, lb_logits, g_norm, state, *, n_streams, seq, hp, chunk, col0, layer):
    hk = lb_logits.shape[1]
    n_heads = hk // HEAD_DIM
    wblk = hp * HEAD_DIM
    seg = hk // wblk
    c0 = col0 // wblk
    has_state = state is not None
    kern = functools.partial(_hgrn_kernel, hp=hp, chunk=chunk, n_chunks=seq // chunk, has_state=has_state,
                             layer=layer)

    def seg_spec(k):
        return pl.BlockSpec((seq, wblk), lambda b, j: (b, c0 + k * seg + j))

    in_specs = [seg_spec(0), seg_spec(1), seg_spec(2), seg_spec(3),
                pl.BlockSpec((lb_logits.shape[0], wblk), lambda b, j: (0, j)),
                pl.BlockSpec((1, HEAD_DIM), lambda b, j: (0, 0))]
    args = [proj, proj, proj, proj, lb_logits, g_norm]
    state_spec = pl.BlockSpec((None, hp, HEAD_DIM, HEAD_DIM), lambda b, j: (b, j, 0, 0))
    if has_state:
        in_specs.append(state_spec)
        args.append(state)
    return pl.pallas_call(
        kern,
        grid=(n_streams, n_heads // hp),
        in_specs=in_specs,
        out_specs=[pl.BlockSpec((seq, wblk), lambda b, j: (b, j)), state_spec],
        out_shape=[
            jax.ShapeDtypeStruct((n_streams * seq, hk), BF16),
            jax.ShapeDtypeStruct((n_streams, n_heads, HEAD_DIM, HEAD_DIM), F32),
        ],
        scratch_shapes=[pltpu.VMEM((hp, HEAD_DIM, HEAD_DIM), F32),
                        pltpu.VMEM((chunk, chunk), BF16),
                        pltpu.VMEM((chunk, chunk), jnp.int32)],
        compiler_params=_params(2),
        name="hgrn_branch",
    )(*args)


def _merge_kernel(ha_ref, ob_ref, ga_ref, gb_ref, bg_ref, wa_ref, wb_ref, m_ref):
    ya = _dot(ha_ref[...], wa_ref[...])
    yb = _dot(ob_ref[...], wb_ref[...])
    ga = _sigmoid(ga_ref[...].astype(F32) + bg_ref[0:1, :])
    gb = _sigmoid(gb_ref[...].astype(F32) + bg_ref[1:2, :])
    m_ref[...] = (ga * ya + gb * yb).astype(m_ref.dtype)


def _merge(h_a, ob, proj, b_gate, w_a, w_b, *, tm, tn, gate_col0):
    n, d_a = h_a.shape
    d_b = ob.shape[1]
    d = w_a.shape[1]
    ga0 = gate_col0 // tn
    gb0 = (gate_col0 + d) // tn
    return pl.pallas_call(
        _merge_kernel,
        grid=(n // tm, d // tn),
        in_specs=[
            pl.BlockSpec((tm, d_a), lambda i, j: (i, 0)),
            pl.BlockSpec((tm, d_b), lambda i, j: (i, 0)),
            pl.BlockSpec((tm, tn), lambda i, j: (i, ga0 + j)),
            pl.BlockSpec((tm, tn), lambda i, j: (i, gb0 + j)),
            pl.BlockSpec((2, tn), lambda i, j: (0, j)),
            pl.BlockSpec((d_a, tn), lambda i, j: (0, j)),
            pl.BlockSpec((d_b, tn), lambda i, j: (0, j)),
        ],
        out_specs=pl.BlockSpec((tm, tn), lambda i, j: (i, j)),
        out_shape=jax.ShapeDtypeStruct((n, d), BF16),
        compiler_params=_params(2),
        name="merge",
    )(h_a, ob, proj, proj, b_gate, w_a, w_b)


def _outproj_kernel(m_ref, x_ref, nw_ref, w_ref, y_ref, *, tn):
    j = pl.program_id(1)
    cols = pl.ds(pl.multiple_of(j * tn, tn), tn)
    y_ref[:, cols] = _dot(m_ref[...], w_ref[...])

    @pl.when(j == pl.num_programs(1) - 1)
    def _():
        out = y_ref[...]
        ms = jnp.mean(out * out, axis=-1, keepdims=True)
        y_ref[...] = x_ref[...] + out * lax.rsqrt(ms + RMS_EPS) * nw_ref[...]


def _outproj(m, x2d, norm_w, w, *, tm, tn):
    n, d = x2d.shape
    return pl.pallas_call(
        functools.partial(_outproj_kernel, tn=tn),
        grid=(n // tm, d // tn),
        in_specs=[
            pl.BlockSpec((tm, d), lambda i, j: (i, 0)),
            pl.BlockSpec((tm, d), lambda i, j: (i, 0)),
            pl.BlockSpec((1, d), lambda i, j: (0, 0)),
            pl.BlockSpec((d, tn), lambda i, j: (0, j)),
        ],
        out_specs=pl.BlockSpec((tm, d), lambda i, j: (i, 0)),
        out_shape=jax.ShapeDtypeStruct((n, d), F32),
        compiler_params=_params(2),
        name="outproj",
    )(m, x2d, norm_w, w)


def _layer(x, pool_buf, state, start_pos, p, layer, *, hgrn_hp, hgrn_chunk, pool_tt, tm_in, tm_out):
    n_streams, seq, d = x.shape
    d_a = p["w_pool"].shape[0] * p["w_pool"].shape[1]
    hk = p["lb_logits"].shape[1]
    x2d = x.reshape(n_streams * seq, d)
    proj = _inproj(x2d, p["norm_pre"], p["w_in"], tm=tm_in, tn=1024)
    buf = jnp.pad(pool_buf.astype(F32), ((0, 0), (POOL_HALO - POOL_BUF, 0), (0, 0)))
    h_a, new_buf = _pool_branch(proj, buf, p["w_pool"], p["pool_scale"],
                                n_streams=n_streams, seq=seq, tt=pool_tt, start_pos=start_pos)
    ob, s_fin = _hgrn_branch(proj, p["lb_logits"], p["g_norm"], state,
                             n_streams=n_streams, seq=seq, hp=hgrn_hp, chunk=hgrn_chunk, col0=2 * d_a,
                             layer=layer)
    m = _merge(h_a, ob, proj, p["b_gate"], p["w_branch_a"], p["w_branch_b"],
               tm=tm_in, tn=1024, gate_col0=2 * d_a + 4 * hk)
    y = _outproj(m, x2d, p["norm_post"], p["w_out"], tm=tm_out, tn=512)
    return y.reshape(x.shape), new_buf, s_fin


def kernel(x_prompt, x_sample, state_pool, state_hgrn, norm_pre, norm_post, w_in, w_pool, pool_scale,
           lb_logits, g_norm, w_branch_a, w_branch_b, b_gate, w_out):
    depth = w_in.shape[0]
    past_len = 1024
    hp, hs = x_prompt, x_sample
    pool_p, hgrn_p, pool_s, hgrn_s = [], [], [], []
    for l in range(depth):
        p = {
            "norm_pre": norm_pre[l][None, :],
            "norm_post": norm_post[l][None, :],
            "w_in": w_in[l].astype(BF16),
            "w_pool": w_pool[l].astype(BF16),
            "pool_scale": pool_scale[l][None, :],
            "lb_logits": lb_logits,
            "g_norm": g_norm[l][None, :],
            "w_branch_a": w_branch_a[l].astype(BF16),
            "w_branch_b": w_branch_b[l].astype(BF16),
            "b_gate": b_gate[l],
            "w_out": w_out[l].astype(BF16),
        }
        buf0 = jnp.zeros((hp.shape[0], POOL_BUF, state_pool.shape[-1]), F32)
        hp, bp, sp = _layer(hp, buf0, None, 0, p, l,
                            hgrn_hp=4, hgrn_chunk=128, pool_tt=512, tm_in=512, tm_out=512)
        hs, bs, ss = _layer(hs, state_pool[l], state_hgrn[l], past_len, p, l,
                            hgrn_hp=8, hgrn_chunk=32, pool_tt=32, tm_in=512, tm_out=512)
        pool_p.append(bp)
        hgrn_p.append(sp)
        pool_s.append(bs)
        hgrn_s.append(ss)
    return (hp, hs, jnp.stack(pool_p), jnp.stack(hgrn_p), jnp.stack(pool_s), jnp.stack(hgrn_s))
```

```python
import functools

import jax
import jax.numpy as jnp
from jax import lax
from jax.experimental import pallas as pl
from jax.experimental.pallas import tpu as pltpu

F32 = jnp.float32
BF16 = jnp.bfloat16

RMS_EPS = 1e-6
POOL_WINDOWS = (2, 4, 8, 16)
POOL_BUF = 15
POOL_HALO = 16
HEAD_DIM = 128
LOG2_E = 1.4426950408889634
V7X_VMEM_LIMIT_BYTES = 56 * 1024 * 1024


def _params(n_axes):
    return pltpu.CompilerParams(
        dimension_semantics=("arbitrary",) * n_axes,
        vmem_limit_bytes=V7X_VMEM_LIMIT_BYTES,
    )


def _sigmoid(x):
    return 1.0 / (1.0 + jnp.exp(-x))


def _silu(x):
    h = 0.5 * x
    return h + h * jnp.tanh(h)


def _dot(a, b):
    return jnp.dot(a, b, preferred_element_type=F32)


def _dot_nt(a, b):
    return lax.dot_general(a, b, (((1,), (1,)), ((), ())), preferred_element_type=F32)


def _dot_tn(a, b):
    return lax.dot_general(a, b, (((0,), (0,)), ((), ())), preferred_element_type=F32)


def _prenorm_kernel(xp_ref, xs_ref, nw_ref, o_ref, *, n_p_tiles):
    def norm(x_ref):
        x = x_ref[...]
        ms = jnp.mean(x * x, axis=-1, keepdims=True)
        o_ref[...] = (x * lax.rsqrt(ms + RMS_EPS) * nw_ref[...]).astype(o_ref.dtype)

    i = pl.program_id(0)

    @pl.when(i < n_p_tiles)
    def _():
        norm(xp_ref)

    @pl.when(i >= n_p_tiles)
    def _():
        norm(xs_ref)


def _prenorm(xp2d, xs2d, norm_w, *, tm):
    n_p, d = xp2d.shape
    n_s = xs2d.shape[0]
    npt, nst = n_p // tm, n_s // tm
    return pl.pallas_call(
        functools.partial(_prenorm_kernel, n_p_tiles=npt),
        grid=(npt + nst,),
        in_specs=[
            pl.BlockSpec((tm, d), lambda i: (jnp.minimum(i, npt - 1), 0)),
            pl.BlockSpec((tm, d), lambda i: (jnp.maximum(i - npt, 0), 0)),
            pl.BlockSpec((1, d), lambda i: (0, 0)),
        ],
        out_specs=pl.BlockSpec((tm, d), lambda i: (i, 0)),
        out_shape=jax.ShapeDtypeStruct((n_p + n_s, d), BF16),
        compiler_params=_params(1),
        name="prenorm",
    )(xp2d, xs2d, norm_w)


def _inproj_kernel(xn_ref, w_ref, o_ref, wb_ref):
    @pl.when(pl.program_id(1) == 0)
    def _():
        wb_ref[...] = w_ref[...].astype(BF16)

    o_ref[...] = _dot(xn_ref[...], wb_ref[...]).astype(o_ref.dtype)


def _inproj(xn, w, *, tm, tn):
    n, d = xn.shape
    e = w.shape[1]
    return pl.pallas_call(
        _inproj_kernel,
        grid=(e // tn, n // tm),
        in_specs=[
            pl.BlockSpec((tm, d), lambda j, i: (i, 0)),
            pl.BlockSpec((d, tn), lambda j, i: (0, j)),
        ],
        out_specs=pl.BlockSpec((tm, tn), lambda j, i: (i, j)),
        out_shape=jax.ShapeDtypeStruct((n, e), BF16),
        scratch_shapes=[pltpu.VMEM((d, tn), BF16)],
        compiler_params=_params(2),
        name="inproj",
    )(xn, w)


def _pool_kernel(u_ref, z_ref, buf_ref, wp_ref, sc_ref, h_ref, nb_ref, prev_ref, *, tt, start_pos):
    t = pl.program_id(1)
    pg = wp_ref.shape[1]

    @pl.when(t == 0)
    def _():
        prev_ref[...] = buf_ref[...]

    u = u_ref[...].astype(F32)
    ext = jnp.concatenate([prev_ref[...], u], axis=0)
    pos = start_pos + t * tt + lax.broadcasted_iota(jnp.int32, (tt, 1), 0)
    for g, w in enumerate(POOL_WINDOWS):
        cs = slice(g * pg, (g + 1) * pg)
        cur = ext[:, cs]
        d = 1
        while d < w:
            cur = cur[d:] + cur[:-d]
            d *= 2
        first = POOL_HALO + 1 - w
        s = cur[first:first + tt]
        cnt = jnp.minimum(w, pos + 1).astype(F32)
        pooled = s / cnt - u[:, cs]
        mixed = _dot(pooled.astype(BF16), wp_ref[g])
        z = z_ref[:, cs].astype(F32)
        h_ref[:, cs] = (mixed * sc_ref[:, cs] * _silu(z)).astype(h_ref.dtype)
    prev_ref[...] = ext[tt:, :]

    @pl.when(t == pl.num_programs(1) - 1)
    def _():
        nb_ref[...] = ext[tt + POOL_HALO - POOL_BUF:, :]


def _pool_branch(proj, buf, w_pool, pool_scale, *, row0, n_streams, seq, tt, start_pos):
    d_a = buf.shape[-1]
    nt = seq // tt
    t0 = row0 // tt
    kern = functools.partial(_pool_kernel, tt=tt, start_pos=start_pos)
    return pl.pallas_call(
        kern,
        grid=(n_streams, nt),
        in_specs=[
            pl.BlockSpec((tt, d_a), lambda b, t: (t0 + b * nt + t, 0)),
            pl.BlockSpec((tt, d_a), lambda b, t: (t0 + b * nt + t, 1)),
            pl.BlockSpec((None, POOL_HALO, d_a), lambda b, t: (b, 0, 0)),
            pl.BlockSpec(w_pool.shape, lambda b, t: (0, 0, 0)),
            pl.BlockSpec((1, d_a), lambda b, t: (0, 0)),
        ],
        out_specs=[
            pl.BlockSpec((tt, d_a), lambda b, t: (b * nt + t, 0)),
            pl.BlockSpec((None, POOL_BUF, d_a), lambda b, t: (b, 0, 0)),
        ],
        out_shape=[
            jax.ShapeDtypeStruct((n_streams * seq, d_a), BF16),
            jax.ShapeDtypeStruct((n_streams, POOL_BUF, d_a), F32),
        ],
        scratch_shapes=[pltpu.VMEM((POOL_HALO, d_a), F32)],
        compiler_params=_params(2),
        name="pool_branch",
    )(proj, proj, buf, w_pool, pool_scale)


def _level_operands(q, k, b, lf, half):
    c = b.shape[0]
    if half >= 8:
        blk = 2 * half
        zeros = jnp.zeros((half, b.shape[1]), BF16)
        qs, ks = [], []
        for i in range(c // blk):
            lo = slice(i * blk, i * blk + half)
            up = slice(i * blk + half, (i + 1) * blk)
            bm = b[i * blk + half - 1:i * blk + half, :]
            qs += [zeros, (q[up] * jnp.exp2(b[up] - bm)).astype(BF16)]
            ks += [(k[lo] * jnp.exp2(bm - b[lo])).astype(BF16), zeros]
        return jnp.concatenate(qs, axis=0), jnp.concatenate(ks, axis=0)
    t = lax.broadcasted_iota(jnp.int32, b.shape, 0)
    if half == 1:
        gap = jnp.where((t & 1) == 1, lf, 0.0)
    elif half == 2:
        r = t & 3
        up = pltpu.roll(lf, 1, 0)
        dn = pltpu.roll(lf, c - 1, 0)
        gap = jnp.where(r == 0, dn, jnp.where(r == 1, 0.0, jnp.where(r == 2, lf, lf + up)))
    else:
        blk = 2 * half
        bm = jnp.concatenate(
            [jnp.broadcast_to(b[i * blk + half - 1:i * blk + half, :], (blk, b.shape[1])) for i in range(c // blk)],
            axis=0)
        gap = -jnp.abs(b - bm)
    e = jnp.exp2(gap)
    return (q * e).astype(BF16), (k * e).astype(BF16)


def _hgrn_chunk(q_raw, f_raw, v, z_raw, st, lb, gn, tri, lvl):
    c = q_raw.shape[0]
    n_levels = c.bit_length() - 1
    q = _silu(q_raw) * (HEAD_DIM ** -0.5)
    forget = lb + (1.0 - lb) * _sigmoid(f_raw)
    k = 1.0 - forget
    lf = jnp.log(forget) * LOG2_E
    hi = lf.astype(BF16)
    lo = (lf - hi.astype(F32)).astype(BF16)
    b = _dot(tri, hi) + _dot(tri, lo)
    b_last = b[c - 1:c, :]
    o = _dot_nt((q * jnp.exp2(b)).astype(BF16), st.astype(BF16))
    a = jnp.where(lvl == n_levels, _dot_nt(q.astype(BF16), k.astype(BF16)), 0.0)
    for lv in range(n_levels):
        half = 1 << lv
        ql, kl = _level_operands(q, k, b, lf, half)
        a_lv = _dot_nt(ql, kl)
        if half < 8:
            a = jnp.where(lvl == lv, a_lv, a)
        else:
            pieces = []
            for i in range(c // (2 * half)):
                lo = slice(2 * i * half, (2 * i + 1) * half)
                up = slice((2 * i + 1) * half, (2 * i + 2) * half)
                pieces += [a[lo], jnp.where(lvl[up] == lv, a_lv[up], a[up])]
            a = jnp.concatenate(pieces, axis=0)
    o = o + _dot(a.astype(BF16), v)
    k_end = (k * jnp.exp2(b_last - b)).astype(BF16)
    st_new = st * jnp.exp2(b_last) + _dot_tn(v, k_end)
    o = o * lax.rsqrt(jnp.mean(o * o, axis=-1, keepdims=True) + RMS_EPS) * gn
    return o * _silu(z_raw), st_new


def _hgrn_kernel(*refs, hp, chunk, n_chunks, has_state, layer):
    if has_state:
        q_ref, f_ref, i_ref, z_ref, lbl_ref, gn_ref, s0_ref, o_ref, sfin_ref, st_ref, tri_ref, lvl_ref = refs
    else:
        q_ref, f_ref, i_ref, z_ref, lbl_ref, gn_ref, o_ref, sfin_ref, st_ref, tri_ref, lvl_ref = refs
    c = chunk
    n_levels = c.bit_length() - 1
    row = lax.broadcasted_iota(jnp.int32, (c, c), 0)
    col = lax.broadcasted_iota(jnp.int32, (c, c), 1)
    tri_ref[...] = (row >= col).astype(BF16)
    lvl_ref[...] = jnp.where(row > col, 31 - lax.clz(row ^ col), jnp.where(row == col, n_levels, -1))
    lg = lbl_ref[...]
    lmax = jnp.max(lg, axis=0, keepdims=True)
    le = jnp.exp(lg - lmax)
    lb_all = (jnp.sum(le[0:layer + 1, :], axis=0, keepdims=True)
              / jnp.sum(le, axis=0, keepdims=True))
    gn = gn_ref[...]

    for h in range(hp):
        if has_state:
            st_ref[h] = s0_ref[h].T
        else:
            st_ref[h] = jnp.zeros((HEAD_DIM, HEAD_DIM), F32)

    def body(ci, carry):
        rows = pl.ds(pl.multiple_of(ci * c, c), c)
        for h in range(hp):
            ls = slice(h * HEAD_DIM, (h + 1) * HEAD_DIM)
            o, st_new = _hgrn_chunk(q_ref[rows, ls].astype(F32), f_ref[rows, ls].astype(F32), i_ref[rows, ls],
                                    z_ref[rows, ls].astype(F32), st_ref[h], lb_all[:, ls], gn,
                                    tri_ref[...], lvl_ref[...])
            st_ref[h] = st_new
            o_ref[rows, ls] = o.astype(o_ref.dtype)
        return carry

    lax.fori_loop(0, n_chunks, body, 0, unroll=min(2, n_chunks))
    for h in range(hp):
        sfin_ref[h] = st_ref[h].T


def _hgrn_branch(proj, lb_logits, g_norm, state, *, row0, n_streams, seq, hp, chunk, col0, layer):
    b0 = row0 // seq
    hk = lb_logits.shape[1]
    n_heads = hk // HEAD_DIM
    wblk = hp * HEAD_DIM
    seg = hk // wblk
    c0 = col0 // wblk
    has_state = state is not None
    kern = functools.partial(_hgrn_kernel, hp=hp, chunk=chunk, n_chunks=seq // chunk, has_state=has_state,
                             layer=layer)

    def seg_spec(k):
        return pl.BlockSpec((seq, wblk), lambda b, j: (b0 + b, c0 + k * seg + j))

    in_specs = [seg_spec(0), seg_spec(1), seg_spec(2), seg_spec(3),
                pl.BlockSpec((lb_logits.shape[0], wblk), lambda b, j: (0, j)),
                pl.BlockSpec((1, HEAD_DIM), lambda b, j: (0, 0))]
    args = [proj, proj, proj, proj, lb_logits, g_norm]
    state_spec = pl.BlockSpec((None, hp, HEAD_DIM, HEAD_DIM), lambda b, j: (b, j, 0, 0))
    if has_state:
        in_specs.append(state_spec)
        args.append(state)
    return pl.pallas_call(
        kern,
        grid=(n_streams, n_heads // hp),
        in_specs=in_specs,
        out_specs=[pl.BlockSpec((seq, wblk), lambda b, j: (b, j)), state_spec],
        out_shape=[
            jax.ShapeDtypeStruct((n_streams * seq, hk), BF16),
            jax.ShapeDtypeStruct((n_streams, n_heads, HEAD_DIM, HEAD_DIM), F32),
        ],
        scratch_shapes=[pltpu.VMEM((hp, HEAD_DIM, HEAD_DIM), F32),
                        pltpu.VMEM((chunk, chunk), BF16),
                        pltpu.VMEM((chunk, chunk), jnp.int32)],
        compiler_params=_params(2),
        name="hgrn_branch",
    )(*args)


def _merge_kernel(ha_ref, ob_ref, ga_ref, gb_ref, bg_ref, wa_ref, wb_ref, m_ref):
    ya = _dot(ha_ref[...], wa_ref[...])
    yb = _dot(ob_ref[...], wb_ref[...])
    ga = _sigmoid(ga_ref[...].astype(F32) + bg_ref[0:1, :])
    gb = _sigmoid(gb_ref[...].astype(F32) + bg_ref[1:2, :])
    m_ref[...] = (ga * ya + gb * yb).astype(m_ref.dtype)


def _merge(h_a, ob, proj, b_gate, w_a, w_b, *, row0, tm, tn, gate_col0):
    n, d_a = h_a.shape
    d_b = ob.shape[1]
    d = w_a.shape[1]
    ga0 = gate_col0 // tn
    gb0 = (gate_col0 + d) // tn
    i0 = row0 // tm
    return pl.pallas_call(
        _merge_kernel,
        grid=(n // tm, d // tn),
        in_specs=[
            pl.BlockSpec((tm, d_a), lambda i, j: (i, 0)),
            pl.BlockSpec((tm, d_b), lambda i, j: (i, 0)),
            pl.BlockSpec((tm, tn), lambda i, j: (i0 + i, ga0 + j)),
            pl.BlockSpec((tm, tn), lambda i, j: (i0 + i, gb0 + j)),
            pl.BlockSpec((2, tn), lambda i, j: (0, j)),
            pl.BlockSpec((d_a, tn), lambda i, j: (0, j)),
            pl.BlockSpec((d_b, tn), lambda i, j: (0, j)),
        ],
        out_specs=pl.BlockSpec((tm, tn), lambda i, j: (i, j)),
        out_shape=jax.ShapeDtypeStruct((n, d), BF16),
        compiler_params=_params(2),
        name="merge",
    )(h_a, ob, proj, proj, b_gate, w_a, w_b)


def _outproj_kernel(m_ref, x_ref, nw_ref, w_ref, y_ref, *, tn):
    j = pl.program_id(1)
    cols = pl.ds(pl.multiple_of(j * tn, tn), tn)
    y_ref[:, cols] = _dot(m_ref[...], w_ref[...])

    @pl.when(j == pl.num_programs(1) - 1)
    def _():
        out = y_ref[...]
        ms = jnp.mean(out * out, axis=-1, keepdims=True)
        y_ref[...] = x_ref[...] + out * lax.rsqrt(ms + RMS_EPS) * nw_ref[...]


def _outproj(m, x2d, norm_w, w, *, tm, tn):
    n, d = x2d.shape
    return pl.pallas_call(
        functools.partial(_outproj_kernel, tn=tn),
        grid=(n // tm, d // tn),
        in_specs=[
            pl.BlockSpec((tm, d), lambda i, j: (i, 0)),
            pl.BlockSpec((tm, d), lambda i, j: (i, 0)),
            pl.BlockSpec((1, d), lambda i, j: (0, 0)),
            pl.BlockSpec((d, tn), lambda i, j: (0, j)),
        ],
        out_specs=pl.BlockSpec((tm, d), lambda i, j: (i, 0)),
        out_shape=jax.ShapeDtypeStruct((n, d), F32),
        compiler_params=_params(2),
        name="outproj",
    )(m, x2d, norm_w, w)


def _mix(x, proj, row0, pool_buf, state, start_pos, p, layer, *, hgrn_hp, hgrn_chunk, pool_tt, tm):
    n_streams, seq, d = x.shape
    d_a = p["w_pool"].shape[0] * p["w_pool"].shape[1]
    hk = p["lb_logits"].shape[1]
    buf = jnp.pad(pool_buf.astype(F32), ((0, 0), (POOL_HALO - POOL_BUF, 0), (0, 0)))
    h_a, new_buf = _pool_branch(proj, buf, p["w_pool"], p["pool_scale"], row0=row0,
                                n_streams=n_streams, seq=seq, tt=pool_tt, start_pos=start_pos)
    ob, s_fin = _hgrn_branch(proj, p["lb_logits"], p["g_norm"], state, row0=row0,
                             n_streams=n_streams, seq=seq, hp=hgrn_hp, chunk=hgrn_chunk, col0=2 * d_a,
                             layer=layer)
    m = _merge(h_a, ob, proj, p["b_gate"], p["w_branch_a"], p["w_branch_b"], row0=row0,
               tm=tm, tn=1024, gate_col0=2 * d_a + 4 * hk)
    y = _outproj(m, x.reshape(n_streams * seq, d), p["norm_post"], p["w_out"], tm=tm, tn=512)
    return y.reshape(x.shape), new_buf, s_fin


def kernel(x_prompt, x_sample, state_pool, state_hgrn, norm_pre, norm_post, w_in, w_pool, pool_scale,
           lb_logits, g_norm, w_branch_a, w_branch_b, b_gate, w_out):
    depth = w_in.shape[0]
    past_len = 1024
    hp, hs = x_prompt, x_sample
    pool_p, hgrn_p, pool_s, hgrn_s = [], [], [], []
    for l in range(depth):
        p = {
            "norm_pre": norm_pre[l][None, :],
            "norm_post": norm_post[l][None, :],
            "w_in": w_in[l],
            "w_pool": w_pool[l].astype(BF16),
            "pool_scale": pool_scale[l][None, :],
            "lb_logits": lb_logits,
            "g_norm": g_norm[l][None, :],
            "w_branch_a": w_branch_a[l].astype(BF16),
            "w_branch_b": w_branch_b[l].astype(BF16),
            "b_gate": b_gate[l],
            "w_out": w_out[l].astype(BF16),
        }
        buf0 = jnp.zeros((hp.shape[0], POOL_BUF, state_pool.shape[-1]), F32)
        d = hp.shape[-1]
        n_p = hp.shape[0] * hp.shape[1]
        xn = _prenorm(hp.reshape(n_p, d), hs.reshape(-1, d), p["norm_pre"], tm=256)
        proj = _inproj(xn, p["w_in"], tm=512, tn=1024)
        hp, bp, sp = _mix(hp, proj, 0, buf0, None, 0, p, l,
                          hgrn_hp=8, hgrn_chunk=128, pool_tt=512, tm=512)
        hs, bs, ss = _mix(hs, proj, n_p, state_pool[l], state_hgrn[l], past_len, p, l,
                          hgrn_hp=16, hgrn_chunk=32, pool_tt=32, tm=512)
        pool_p.append(bp)
        hgrn_p.append(sp)
        pool_s.append(bs)
        hgrn_s.append(ss)
    return (hp, hs, jnp.stack(pool_p), jnp.stack(hgrn_p), jnp.stack(pool_s), jnp.stack(hgrn_s))
```

```python
import functools

import jax
import jax.numpy as jnp
from jax import lax
from jax.experimental import pallas as pl
from jax.experimental.pallas import tpu as pltpu

F32 = jnp.float32
BF16 = jnp.bfloat16

RMS_EPS = 1e-6
POOL_WINDOWS = (2, 4, 8, 16)
POOL_BUF = 15
POOL_HALO = 16
HEAD_DIM = 128
LOG2_E = 1.4426950408889634
V7X_VMEM_LIMIT_BYTES = 56 * 1024 * 1024


def _params(n_axes):
    return pltpu.CompilerParams(
        dimension_semantics=("arbitrary",) * n_axes,
        vmem_limit_bytes=V7X_VMEM_LIMIT_BYTES,
    )


def _sigmoid(x):
    return 1.0 / (1.0 + jnp.exp(-x))


def _silu(x):
    h = 0.5 * x
    return h + h * jnp.tanh(h)


def _dot(a, b):
    return jnp.dot(a, b, preferred_element_type=F32)


def _dot_nt(a, b):
    return lax.dot_general(a, b, (((1,), (1,)), ((), ())), preferred_element_type=F32)


def _dot_tn(a, b):
    return lax.dot_general(a, b, (((0,), (0,)), ((), ())), preferred_element_type=F32)


def _prenorm_kernel(xp_ref, xs_ref, nw_ref, o_ref, *, n_p_tiles):
    def norm(x_ref):
        x = x_ref[...]
        ms = jnp.mean(x * x, axis=-1, keepdims=True)
        o_ref[...] = (x * lax.rsqrt(ms + RMS_EPS) * nw_ref[...]).astype(o_ref.dtype)

    i = pl.program_id(0)

    @pl.when(i < n_p_tiles)
    def _():
        norm(xp_ref)

    @pl.when(i >= n_p_tiles)
    def _():
        norm(xs_ref)


def _prenorm(xp2d, xs2d, norm_w, *, tm):
    n_p, d = xp2d.shape
    n_s = xs2d.shape[0]
    npt, nst = n_p // tm, n_s // tm
    return pl.pallas_call(
        functools.partial(_prenorm_kernel, n_p_tiles=npt),
        grid=(npt + nst,),
        in_specs=[
            pl.BlockSpec((tm, d), lambda i: (jnp.minimum(i, npt - 1), 0)),
            pl.BlockSpec((tm, d), lambda i: (jnp.maximum(i - npt, 0), 0)),
            pl.BlockSpec((1, d), lambda i: (0, 0)),
        ],
        out_specs=pl.BlockSpec((tm, d), lambda i: (i, 0)),
        out_shape=jax.ShapeDtypeStruct((n_p + n_s, d), BF16),
        compiler_params=_params(1),
        name="prenorm",
    )(xp2d, xs2d, norm_w)


def _inproj_kernel(xn_ref, w_ref, o_ref, wb_ref):
    @pl.when(pl.program_id(1) == 0)
    def _():
        wb_ref[...] = w_ref[...].astype(BF16)

    o_ref[...] = _dot(xn_ref[...], wb_ref[...]).astype(o_ref.dtype)


def _inproj(xn, w, *, tm, tn):
    n, d = xn.shape
    e = w.shape[1]
    return pl.pallas_call(
        _inproj_kernel,
        grid=(e // tn, n // tm),
        in_specs=[
            pl.BlockSpec((tm, d), lambda j, i: (i, 0)),
            pl.BlockSpec((d, tn), lambda j, i: (0, j)),
        ],
        out_specs=pl.BlockSpec((tm, tn), lambda j, i: (i, j)),
        out_shape=jax.ShapeDtypeStruct((n, e), BF16),
        scratch_shapes=[pltpu.VMEM((d, tn), BF16)],
        compiler_params=_params(2),
        name="inproj",
    )(xn, w)


def _pool_kernel(u_ref, z_ref, buf_ref, wp_ref, sc_ref, h_ref, nb_ref, prev_ref, *, tt, start_pos):
    t = pl.program_id(1)
    pg = wp_ref.shape[1]

    @pl.when(t == 0)
    def _():
        prev_ref[...] = buf_ref[...]

    u = u_ref[...].astype(F32)
    ext = jnp.concatenate([prev_ref[...], u], axis=0)
    pos = start_pos + t * tt + lax.broadcasted_iota(jnp.int32, (tt, 1), 0)
    for g, w in enumerate(POOL_WINDOWS):
        cs = slice(g * pg, (g + 1) * pg)
        cur = ext[:, cs]
        d = 1
        while d < w:
            cur = cur[d:] + cur[:-d]
            d *= 2
        first = POOL_HALO + 1 - w
        s = cur[first:first + tt]
        cnt = jnp.minimum(w, pos + 1).astype(F32)
        pooled = s / cnt - u[:, cs]
        mixed = _dot(pooled.astype(BF16), wp_ref[g])
        z = z_ref[:, cs].astype(F32)
        h_ref[:, cs] = (mixed * sc_ref[:, cs] * _silu(z)).astype(h_ref.dtype)
    prev_ref[...] = ext[tt:, :]

    @pl.when(t == pl.num_programs(1) - 1)
    def _():
        nb_ref[...] = ext[tt + POOL_HALO - POOL_BUF:, :]


def _pool_branch(proj, buf, w_pool, pool_scale, *, row0, n_streams, seq, tt, start_pos):
    d_a = buf.shape[-1]
    nt = seq // tt
    t0 = row0 // tt
    kern = functools.partial(_pool_kernel, tt=tt, start_pos=start_pos)
    return pl.pallas_call(
        kern,
        grid=(n_streams, nt),
        in_specs=[
            pl.BlockSpec((tt, d_a), lambda b, t: (t0 + b * nt + t, 0)),
            pl.BlockSpec((tt, d_a), lambda b, t: (t0 + b * nt + t, 1)),
            pl.BlockSpec((None, POOL_HALO, d_a), lambda b, t: (b, 0, 0)),
            pl.BlockSpec(w_pool.shape, lambda b, t: (0, 0, 0)),
            pl.BlockSpec((1, d_a), lambda b, t: (0, 0)),
        ],
        out_specs=[
            pl.BlockSpec((tt, d_a), lambda b, t: (b * nt + t, 0)),
            pl.BlockSpec((None, POOL_BUF, d_a), lambda b, t: (b, 0, 0)),
        ],
        out_shape=[
            jax.ShapeDtypeStruct((n_streams * seq, d_a), BF16),
            jax.ShapeDtypeStruct((n_streams, POOL_BUF, d_a), F32),
        ],
        scratch_shapes=[pltpu.VMEM((POOL_HALO, d_a), F32)],
        compiler_params=_params(2),
        name="pool_branch",
    )(proj, proj, buf, w_pool, pool_scale)


def _level_operands(q, k, b, lf, half):
    c = b.shape[0]
    if half >= 8:
        blk = 2 * half
        zeros = jnp.zeros((half, b.shape[1]), BF16)
        qs, ks = [], []
        for i in range(c // blk):
            lo = slice(i * blk, i * blk + half)
            up = slice(i * blk + half, (i + 1) * blk)
            bm = b[i * blk + half - 1:i * blk + half, :]
            qs += [zeros, (q[up] * jnp.exp2(b[up] - bm)).astype(BF16)]
            ks += [(k[lo] * jnp.exp2(bm - b[lo])).astype(BF16), zeros]
        return jnp.concatenate(qs, axis=0), jnp.concatenate(ks, axis=0)
    t = lax.broadcasted_iota(jnp.int32, b.shape, 0)
    if half == 1:
        gap = jnp.where((t & 1) == 1, lf, 0.0)
    elif half == 2:
        r = t & 3
        up = pltpu.roll(lf, 1, 0)
        dn = pltpu.roll(lf, c - 1, 0)
        gap = jnp.where(r == 0, dn, jnp.where(r == 1, 0.0, jnp.where(r == 2, lf, lf + up)))
    else:
        blk = 2 * half
        bm = jnp.concatenate(
            [jnp.broadcast_to(b[i * blk + half - 1:i * blk + half, :], (blk, b.shape[1])) for i in range(c // blk)],
            axis=0)
        gap = -jnp.abs(b - bm)
    e = jnp.exp2(gap)
    return (q * e).astype(BF16), (k * e).astype(BF16)


def _hgrn_chunk(q_raw, f_raw, v, z_raw, st, lb, gn, tri, lvl):
    c = q_raw.shape[0]
    n_levels = c.bit_length() - 1
    q = _silu(q_raw) * (HEAD_DIM ** -0.5)
    forget = lb + (1.0 - lb) * _sigmoid(f_raw)
    k = 1.0 - forget
    lf = jnp.log(forget) * LOG2_E
    hi = lf.astype(BF16)
    lo = (lf - hi.astype(F32)).astype(BF16)
    b = _dot(tri, hi) + _dot(tri, lo)
    b_last = b[c - 1:c, :]
    o = _dot_nt((q * jnp.exp2(b)).astype(BF16), st.astype(BF16))
    a = jnp.where(lvl == n_levels, _dot_nt(q.astype(BF16), k.astype(BF16)), 0.0)
    for lv in range(n_levels):
        half = 1 << lv
        ql, kl = _level_operands(q, k, b, lf, half)
        a_lv = _dot_nt(ql, kl)
        if half < 8:
            a = jnp.where(lvl == lv, a_lv, a)
        else:
            pieces = []
            for i in range(c // (2 * half)):
                lo = slice(2 * i * half, (2 * i + 1) * half)
                up = slice((2 * i + 1) * half, (2 * i + 2) * half)
                pieces += [a[lo], jnp.where(lvl[up] == lv, a_lv[up], a[up])]
            a = jnp.concatenate(pieces, axis=0)
    o = o + _dot(a.astype(BF16), v)
    k_end = (k * jnp.exp2(b_last - b)).astype(BF16)
    st_new = st * jnp.exp2(b_last) + _dot_tn(v, k_end)
    o = o * lax.rsqrt(jnp.mean(o * o, axis=-1, keepdims=True) + RMS_EPS) * gn
    return o * _silu(z_raw), st_new


def _hgrn_kernel(*refs, hp, chunk, n_chunks, has_state, layer):
    if has_state:
        q_ref, f_ref, i_ref, z_ref, lbl_ref, gn_ref, s0_ref, o_ref, sfin_ref, st_ref, tri_ref, lvl_ref = refs
    else:
        q_ref, f_ref, i_ref, z_ref, lbl_ref, gn_ref, o_ref, sfin_ref, st_ref, tri_ref, lvl_ref = refs
    c = chunk
    n_levels = c.bit_length() - 1
    row = lax.broadcasted_iota(jnp.int32, (c, c), 0)
    col = lax.broadcasted_iota(jnp.int32, (c, c), 1)
    tri_ref[...] = (row >= col).astype(BF16)
    lvl_ref[...] = jnp.where(row > col, 31 - lax.clz(row ^ col), jnp.where(row == col, n_levels, -1))
    lg = lbl_ref[...]
    lmax = jnp.max(lg, axis=0, keepdims=True)
    le = jnp.exp(lg - lmax)
    lb_all = (jnp.sum(le[0:layer + 1, :], axis=0, keepdims=True)
              / jnp.sum(le, axis=0, keepdims=True))
    gn = gn_ref[...]

    for h in range(hp):
        if has_state:
            st_ref[h] = s0_ref[h].T
        else:
            st_ref[h] = jnp.zeros((HEAD_DIM, HEAD_DIM), F32)

    def body(ci, carry):
        rows = pl.ds(pl.multiple_of(ci * c, c), c)
        for h in range(hp):
            ls = slice(h * HEAD_DIM, (h + 1) * HEAD_DIM)
            o, st_new = _hgrn_chunk(q_ref[rows, ls].astype(F32), f_ref[rows, ls].astype(F32), i_ref[rows, ls],
                                    z_ref[rows, ls].astype(F32), st_ref[h], lb_all[:, ls], gn,
                                    tri_ref[...], lvl_ref[...])
            st_ref[h] = st_new
            o_ref[rows, ls] = o.astype(o_ref.dtype)
        return carry

    lax.fori_loop(0, n_chunks, body, 0, unroll=min(2, n_chunks))
    for h in range(hp):
        sfin_ref[h] = st_ref[h].T


def _hgrn_branch(proj, lb_logits, g_norm, state, *, row0, n_streams, seq, hp, chunk, col0, layer):
    b0 = row0 // seq
    hk = lb_logits.shape[1]
    n_heads = hk // HEAD_DIM
    wblk = hp * HEAD_DIM
    seg = hk // wblk
    c0 = col0 // wblk
    has_state = state is not None
    kern = functools.partial(_hgrn_kernel, hp=hp, chunk=chunk, n_chunks=seq // chunk, has_state=has_state,
                             layer=layer)

    def seg_spec(k):
        return pl.BlockSpec((seq, wblk), lambda b, j: (b0 + b, c0 + k * seg + j))

    in_specs = [seg_spec(0), seg_spec(1), seg_spec(2), seg_spec(3),
                pl.BlockSpec((lb_logits.shape[0], wblk), lambda b, j: (0, j)),
                pl.BlockSpec((1, HEAD_DIM), lambda b, j: (0, 0))]
    args = [proj, proj, proj, proj, lb_logits, g_norm]
    state_spec = pl.BlockSpec((None, hp, HEAD_DIM, HEAD_DIM), lambda b, j: (b, j, 0, 0))
    if has_state:
        in_specs.append(state_spec)
        args.append(state)
    return pl.pallas_call(
        kern,
        grid=(n_streams, n_heads // hp),
        in_specs=in_specs,
        out_specs=[pl.BlockSpec((seq, wblk), lambda b, j: (b, j)), state_spec],
        out_shape=[
            jax.ShapeDtypeStruct((n_streams * seq, hk), BF16),
            jax.ShapeDtypeStruct((n_streams, n_heads, HEAD_DIM, HEAD_DIM), F32),
        ],
        scratch_shapes=[pltpu.VMEM((hp, HEAD_DIM, HEAD_DIM), F32),
                        pltpu.VMEM((chunk, chunk), BF16),
                        pltpu.VMEM((chunk, chunk), jnp.int32)],
        compiler_params=_params(2),
        name="hgrn_branch",
    )(*args)


def _merge_kernel(ha_ref, ob_ref, ga_ref, gb_ref, bg_ref, wa_ref, wb_ref, m_ref, wa16_ref, wb16_ref):
    @pl.when(pl.program_id(1) == 0)
    def _():
        wa16_ref[...] = wa_ref[...].astype(BF16)
        wb16_ref[...] = wb_ref[...].astype(BF16)

    ya = _dot(ha_ref[...], wa16_ref[...])
    yb = _dot(ob_ref[...], wb16_ref[...])
    ga = _sigmoid(ga_ref[...].astype(F32) + bg_ref[0:1, :])
    gb = _sigmoid(gb_ref[...].astype(F32) + bg_ref[1:2, :])
    m_ref[...] = (ga * ya + gb * yb).astype(m_ref.dtype)


def _merge(h_a, ob, proj, b_gate, w_a, w_b, *, row0, tm, tn, gate_col0):
    n, d_a = h_a.shape
    d_b = ob.shape[1]
    d = w_a.shape[1]
    ga0 = gate_col0 // tn
    gb0 = (gate_col0 + d) // tn
    i0 = row0 // tm
    return pl.pallas_call(
        _merge_kernel,
        grid=(d // tn, n // tm),
        in_specs=[
            pl.BlockSpec((tm, d_a), lambda j, i: (i, 0)),
            pl.BlockSpec((tm, d_b), lambda j, i: (i, 0)),
            pl.BlockSpec((tm, tn), lambda j, i: (i0 + i, ga0 + j)),
            pl.BlockSpec((tm, tn), lambda j, i: (i0 + i, gb0 + j)),
            pl.BlockSpec((2, tn), lambda j, i: (0, j)),
            pl.BlockSpec((d_a, tn), lambda j, i: (0, j)),
            pl.BlockSpec((d_b, tn), lambda j, i: (0, j)),
        ],
        out_specs=pl.BlockSpec((tm, tn), lambda j, i: (i, j)),
        out_shape=jax.ShapeDtypeStruct((n, d), BF16),
        scratch_shapes=[pltpu.VMEM((d_a, tn), BF16), pltpu.VMEM((d_b, tn), BF16)],
        compiler_params=_params(2),
        name="merge",
    )(h_a, ob, proj, proj, b_gate, w_a, w_b)


def _outproj_kernel(m_ref, x_ref, nw_ref, w_ref, y_ref, *, tn):
    j = pl.program_id(1)
    cols = pl.ds(pl.multiple_of(j * tn, tn), tn)
    y_ref[:, cols] = _dot(m_ref[...], w_ref[...])

    @pl.when(j == pl.num_programs(1) - 1)
    def _():
        out = y_ref[...]
        ms = jnp.mean(out * out, axis=-1, keepdims=True)
        y_ref[...] = x_ref[...] + out * lax.rsqrt(ms + RMS_EPS) * nw_ref[...]


def _outproj(m, x2d, norm_w, w, *, tm, tn):
    n, d = x2d.shape
    return pl.pallas_call(
        functools.partial(_outproj_kernel, tn=tn),
        grid=(n // tm, d // tn),
        in_specs=[
            pl.BlockSpec((tm, d), lambda i, j: (i, 0)),
            pl.BlockSpec((tm, d), lambda i, j: (i, 0)),
            pl.BlockSpec((1, d), lambda i, j: (0, 0)),
            pl.BlockSpec((d, tn), lambda i, j: (0, j)),
        ],
        out_specs=pl.BlockSpec((tm, d), lambda i, j: (i, 0)),
        out_shape=jax.ShapeDtypeStruct((n, d), F32),
        compiler_params=_params(2),
        name="outproj",
    )(m, x2d, norm_w, w)


def _mix(x, proj, row0, pool_buf, state, start_pos, p, layer, *, hgrn_hp, hgrn_chunk, pool_tt, tm):
    n_streams, seq, d = x.shape
    d_a = p["w_pool"].shape[0] * p["w_pool"].shape[1]
    hk = p["lb_logits"].shape[1]
    buf = jnp.pad(pool_buf.astype(F32), ((0, 0), (POOL_HALO - POOL_BUF, 0), (0, 0)))
    h_a, new_buf = _pool_branch(proj, buf, p["w_pool"], p["pool_scale"], row0=row0,
                                n_streams=n_streams, seq=seq, tt=pool_tt, start_pos=start_pos)
    ob, s_fin = _hgrn_branch(proj, p["lb_logits"], p["g_norm"], state, row0=row0,
                             n_streams=n_streams, seq=seq, hp=hgrn_hp, chunk=hgrn_chunk, col0=2 * d_a,
                             layer=layer)
    m = _merge(h_a, ob, proj, p["b_gate"], p["w_branch_a"], p["w_branch_b"], row0=row0,
               tm=tm, tn=512, gate_col0=2 * d_a + 4 * hk)
    y = _outproj(m, x.reshape(n_streams * seq, d), p["norm_post"], p["w_out"], tm=tm, tn=512)
    return y.reshape(x.shape), new_buf, s_fin


def kernel(x_prompt, x_sample, state_pool, state_hgrn, norm_pre, norm_post, w_in, w_pool, pool_scale,
           lb_logits, g_norm, w_branch_a, w_branch_b, b_gate, w_out):
    depth = w_in.shape[0]
    past_len = 1024
    hp, hs = x_prompt, x_sample
    pool_p, hgrn_p, pool_s, hgrn_s = [], [], [], []
    for l in range(depth):
        p = {
            "norm_pre": norm_pre[l][None, :],
            "norm_post": norm_post[l][None, :],
            "w_in": w_in[l],
            "w_pool": w_pool[l].astype(BF16),
            "pool_scale": pool_scale[l][None, :],
            "lb_logits": lb_logits,
            "g_norm": g_norm[l][None, :],
            "w_branch_a": w_branch_a[l],
            "w_branch_b": w_branch_b[l],
            "b_gate": b_gate[l],
            "w_out": w_out[l].astype(BF16),
        }
        buf0 = jnp.zeros((hp.shape[0], POOL_BUF, state_pool.shape[-1]), F32)
        d = hp.shape[-1]
        n_p = hp.shape[0] * hp.shape[1]
        xn = _prenorm(hp.reshape(n_p, d), hs.reshape(-1, d), p["norm_pre"], tm=256)
        proj = _inproj(xn, p["w_in"], tm=512, tn=1024)
        hp, bp, sp = _mix(hp, proj, 0, buf0, None, 0, p, l,
                          hgrn_hp=8, hgrn_chunk=128, pool_tt=512, tm=512)
        hs, bs, ss = _mix(hs, proj, n_p, state_pool[l], state_hgrn[l], past_len, p, l,
                          hgrn_hp=32, hgrn_chunk=32, pool_tt=32, tm=512)
        pool_p.append(bp)
        hgrn_p.append(sp)
        pool_s.append(bs)
        hgrn_s.append(ss)
    return (hp, hs, jnp.stack(pool_p), jnp.stack(hgrn_p), jnp.stack(pool_s), jnp.stack(hgrn_s))
```

```python
import functools

import jax
import jax.numpy as jnp
from jax import lax
from jax.experimental import pallas as pl
from jax.experimental.pallas import tpu as pltpu

F32 = jnp.float32
BF16 = jnp.bfloat16

RMS_EPS = 1e-6
POOL_WINDOWS = (2, 4, 8, 16)
POOL_BUF = 15
POOL_HALO = 16
HEAD_DIM = 128
LOG2_E = 1.4426950408889634
V7X_VMEM_LIMIT_BYTES = 56 * 1024 * 1024


def _params(n_axes):
    return pltpu.CompilerParams(
        dimension_semantics=("arbitrary",) * n_axes,
        vmem_limit_bytes=V7X_VMEM_LIMIT_BYTES,
    )


def _sigmoid(x):
    return 1.0 / (1.0 + jnp.exp(-x))


def _silu(x):
    h = 0.5 * x
    return h + h * jnp.tanh(h)


def _dot(a, b):
    return jnp.dot(a, b, preferred_element_type=F32)


def _dot_nt(a, b):
    return lax.dot_general(a, b, (((1,), (1,)), ((), ())), preferred_element_type=F32)


def _dot_tn(a, b):
    return lax.dot_general(a, b, (((0,), (0,)), ((), ())), preferred_element_type=F32)


def _prenorm_kernel(xp_ref, xs_ref, nw_ref, o_ref, *, n_p_tiles):
    def norm(x_ref):
        x = x_ref[...]
        ms = jnp.mean(x * x, axis=-1, keepdims=True)
        o_ref[...] = (x * lax.rsqrt(ms + RMS_EPS) * nw_ref[...]).astype(o_ref.dtype)

    i = pl.program_id(0)

    @pl.when(i < n_p_tiles)
    def _():
        norm(xp_ref)

    @pl.when(i >= n_p_tiles)
    def _():
        norm(xs_ref)


def _prenorm(xp2d, xs2d, norm_w, *, tm):
    n_p, d = xp2d.shape
    n_s = xs2d.shape[0]
    npt, nst = n_p // tm, n_s // tm
    return pl.pallas_call(
        functools.partial(_prenorm_kernel, n_p_tiles=npt),
        grid=(npt + nst,),
        in_specs=[
            pl.BlockSpec((tm, d), lambda i: (jnp.minimum(i, npt - 1), 0)),
            pl.BlockSpec((tm, d), lambda i: (jnp.maximum(i - npt, 0), 0)),
            pl.BlockSpec((1, d), lambda i: (0, 0)),
        ],
        out_specs=pl.BlockSpec((tm, d), lambda i: (i, 0)),
        out_shape=jax.ShapeDtypeStruct((n_p + n_s, d), BF16),
        compiler_params=_params(1),
        name="prenorm",
    )(xp2d, xs2d, norm_w)


def _inproj_kernel(xn_ref, w_ref, o_ref, wb_ref):
    @pl.when(pl.program_id(1) == 0)
    def _():
        wb_ref[...] = w_ref[...].astype(BF16)

    o_ref[...] = _dot(xn_ref[...], wb_ref[...]).astype(o_ref.dtype)


def _inproj(xn, w, *, tm, tn):
    n, d = xn.shape
    e = w.shape[1]
    return pl.pallas_call(
        _inproj_kernel,
        grid=(e // tn, n // tm),
        in_specs=[
            pl.BlockSpec((tm, d), lambda j, i: (i, 0)),
            pl.BlockSpec((d, tn), lambda j, i: (0, j)),
        ],
        out_specs=pl.BlockSpec((tm, tn), lambda j, i: (i, j)),
        out_shape=jax.ShapeDtypeStruct((n, e), BF16),
        scratch_shapes=[pltpu.VMEM((d, tn), BF16)],
        compiler_params=_params(2),
        name="inproj",
    )(xn, w)


def _pool_kernel(u_ref, z_ref, buf_ref, wp_ref, sc_ref, h_ref, nb_ref, prev_ref, *, tt, start_pos):
    t = pl.program_id(1)
    pg = wp_ref.shape[1]

    @pl.when(t == 0)
    def _():
        prev_ref[...] = buf_ref[...]

    u = u_ref[...].astype(F32)
    ext = jnp.concatenate([prev_ref[...], u], axis=0)
    pos = start_pos + t * tt + lax.broadcasted_iota(jnp.int32, (tt, 1), 0)
    for g, w in enumerate(POOL_WINDOWS):
        cs = slice(g * pg, (g + 1) * pg)
        cur = ext[:, cs]
        d = 1
        while d < w:
            cur = cur[d:] + cur[:-d]
            d *= 2
        first = POOL_HALO + 1 - w
        s = cur[first:first + tt]
        cnt = jnp.minimum(w, pos + 1).astype(F32)
        pooled = s / cnt - u[:, cs]
        mixed = _dot(pooled.astype(BF16), wp_ref[g])
        z = z_ref[:, cs].astype(F32)
        h_ref[:, cs] = (mixed * sc_ref[:, cs] * _silu(z)).astype(h_ref.dtype)
    prev_ref[...] = ext[tt:, :]

    @pl.when(t == pl.num_programs(1) - 1)
    def _():
        nb_ref[...] = ext[tt + POOL_HALO - POOL_BUF:, :]


def _pool_branch(proj, buf, w_pool, pool_scale, *, row0, n_streams, seq, tt, start_pos):
    d_a = buf.shape[-1]
    nt = seq // tt
    t0 = row0 // tt
    kern = functools.partial(_pool_kernel, tt=tt, start_pos=start_pos)
    return pl.pallas_call(
        kern,
        grid=(n_streams, nt),
        in_specs=[
            pl.BlockSpec((tt, d_a), lambda b, t: (t0 + b * nt + t, 0)),
            pl.BlockSpec((tt, d_a), lambda b, t: (t0 + b * nt + t, 1)),
            pl.BlockSpec((None, POOL_HALO, d_a), lambda b, t: (b, 0, 0)),
            pl.BlockSpec(w_pool.shape, lambda b, t: (0, 0, 0)),
            pl.BlockSpec((1, d_a), lambda b, t: (0, 0)),
        ],
        out_specs=[
            pl.BlockSpec((tt, d_a), lambda b, t: (b * nt + t, 0)),
            pl.BlockSpec((None, POOL_BUF, d_a), lambda b, t: (b, 0, 0)),
        ],
        out_shape=[
            jax.ShapeDtypeStruct((n_streams * seq, d_a), BF16),
            jax.ShapeDtypeStruct((n_streams, POOL_BUF, d_a), F32),
        ],
        scratch_shapes=[pltpu.VMEM((POOL_HALO, d_a), F32)],
        compiler_params=_params(2),
        name="pool_branch",
    )(proj, proj, buf, w_pool, pool_scale)


def _level_operands(q, k, b, lf, half):
    c = b.shape[0]
    if half >= 8:
        blk = 2 * half
        zeros = jnp.zeros((half, b.shape[1]), BF16)
        qs, ks = [], []
        for i in range(c // blk):
            lo = slice(i * blk, i * blk + half)
            up = slice(i * blk + half, (i + 1) * blk)
            bm = b[i * blk + half - 1:i * blk + half, :]
            qs += [zeros, (q[up] * jnp.exp2(b[up] - bm)).astype(BF16)]
            ks += [(k[lo] * jnp.exp2(bm - b[lo])).astype(BF16), zeros]
        return jnp.concatenate(qs, axis=0), jnp.concatenate(ks, axis=0)
    t = lax.broadcasted_iota(jnp.int32, b.shape, 0)
    if half == 1:
        gap = jnp.where((t & 1) == 1, lf, 0.0)
    elif half == 2:
        r = t & 3
        up = pltpu.roll(lf, 1, 0)
        dn = pltpu.roll(lf, c - 1, 0)
        gap = jnp.where(r == 0, dn, jnp.where(r == 1, 0.0, jnp.where(r == 2, lf, lf + up)))
    else:
        blk = 2 * half
        bm = jnp.concatenate(
            [jnp.broadcast_to(b[i * blk + half - 1:i * blk + half, :], (blk, b.shape[1])) for i in range(c // blk)],
            axis=0)
        gap = -jnp.abs(b - bm)
    e = jnp.exp2(gap)
    return (q * e).astype(BF16), (k * e).astype(BF16)


def _hgrn_chunk(q_raw, f_raw, v, z_raw, st, lb, gn, tri, lvl):
    c = q_raw.shape[0]
    n_levels = c.bit_length() - 1
    q = _silu(q_raw) * (HEAD_DIM ** -0.5)
    forget = lb + (1.0 - lb) * _sigmoid(f_raw)
    k = 1.0 - forget
    lf = jnp.log(forget) * LOG2_E
    hi = lf.astype(BF16)
    lo = (lf - hi.astype(F32)).astype(BF16)
    b = _dot(tri, hi) + _dot(tri, lo)
    b_last = b[c - 1:c, :]
    o = _dot_nt((q * jnp.exp2(b)).astype(BF16), st.astype(BF16))
    a = jnp.where(lvl == n_levels, _dot_nt(q.astype(BF16), k.astype(BF16)), 0.0)
    for lv in range(n_levels):
        half = 1 << lv
        ql, kl = _level_operands(q, k, b, lf, half)
        a_lv = _dot_nt(ql, kl)
        if half < 8:
            a = jnp.where(lvl == lv, a_lv, a)
        else:
            pieces = []
            for i in range(c // (2 * half)):
                lo = slice(2 * i * half, (2 * i + 1) * half)
                up = slice((2 * i + 1) * half, (2 * i + 2) * half)
                pieces += [a[lo], jnp.where(lvl[up] == lv, a_lv[up], a[up])]
            a = jnp.concatenate(pieces, axis=0)
    o = o + _dot(a.astype(BF16), v)
    k_end = (k * jnp.exp2(b_last - b)).astype(BF16)
    st_new = st * jnp.exp2(b_last) + _dot_tn(v, k_end)
    o = o * lax.rsqrt(jnp.mean(o * o, axis=-1, keepdims=True) + RMS_EPS) * gn
    return o * _silu(z_raw), st_new


def _hgrn_kernel(*refs, hp, chunk, n_chunks, has_state, layer):
    if has_state:
        q_ref, f_ref, i_ref, z_ref, lbl_ref, gn_ref, s0_ref, o_ref, sfin_ref, st_ref, tri_ref, lvl_ref = refs
    else:
        q_ref, f_ref, i_ref, z_ref, lbl_ref, gn_ref, o_ref, sfin_ref, st_ref, tri_ref, lvl_ref = refs
    c = chunk
    n_levels = c.bit_length() - 1
    row = lax.broadcasted_iota(jnp.int32, (c, c), 0)
    col = lax.broadcasted_iota(jnp.int32, (c, c), 1)
    tri_ref[...] = (row >= col).astype(BF16)
    lvl_ref[...] = jnp.where(row > col, 31 - lax.clz(row ^ col), jnp.where(row == col, n_levels, -1))
    lg = lbl_ref[...]
    lmax = jnp.max(lg, axis=0, keepdims=True)
    le = jnp.exp(lg - lmax)
    lb_all = (jnp.sum(le[0:layer + 1, :], axis=0, keepdims=True)
              / jnp.sum(le, axis=0, keepdims=True))
    gn = gn_ref[...]

    for h in range(hp):
        if has_state:
            st_ref[h] = s0_ref[h].T
        else:
            st_ref[h] = jnp.zeros((HEAD_DIM, HEAD_DIM), F32)

    def body(ci, carry):
        rows = pl.ds(pl.multiple_of(ci * c, c), c)
        for h in range(hp):
            ls = slice(h * HEAD_DIM, (h + 1) * HEAD_DIM)
            o, st_new = _hgrn_chunk(q_ref[rows, ls].astype(F32), f_ref[rows, ls].astype(F32), i_ref[rows, ls],
                                    z_ref[rows, ls].astype(F32), st_ref[h], lb_all[:, ls], gn,
                                    tri_ref[...], lvl_ref[...])
            st_ref[h] = st_new
            o_ref[rows, ls] = o.astype(o_ref.dtype)
        return carry

    lax.fori_loop(0, n_chunks, body, 0, unroll=min(4, n_chunks))
    for h in range(hp):
        sfin_ref[h] = st_ref[h].T


def _hgrn_branch(proj, lb_logits, g_norm, state, *, row0, n_streams, seq, hp, chunk, col0, layer):
    b0 = row0 // seq
    hk = lb_logits.shape[1]
    n_heads = hk // HEAD_DIM
    wblk = hp * HEAD_DIM
    seg = hk // wblk
    c0 = col0 // wblk
    has_state = state is not None
    kern = functools.partial(_hgrn_kernel, hp=hp, chunk=chunk, n_chunks=seq // chunk, has_state=has_state,
                             layer=layer)

    def seg_spec(k):
        return pl.BlockSpec((seq, wblk), lambda b, j: (b0 + b, c0 + k * seg + j))

    in_specs = [seg_spec(0), seg_spec(1), seg_spec(2), seg_spec(3),
                pl.BlockSpec((lb_logits.shape[0], wblk), lambda b, j: (0, j)),
                pl.BlockSpec((1, HEAD_DIM), lambda b, j: (0, 0))]
    args = [proj, proj, proj, proj, lb_logits, g_norm]
    state_spec = pl.BlockSpec((None, hp, HEAD_DIM, HEAD_DIM), lambda b, j: (b, j, 0, 0))
    if has_state:
        in_specs.append(state_spec)
        args.append(state)
    return pl.pallas_call(
        kern,
        grid=(n_streams, n_heads // hp),
        in_specs=in_specs,
        out_specs=[pl.BlockSpec((seq, wblk), lambda b, j: (b, j)), state_spec],
        out_shape=[
            jax.ShapeDtypeStruct((n_streams * seq, hk), BF16),
            jax.ShapeDtypeStruct((n_streams, n_heads, HEAD_DIM, HEAD_DIM), F32),
        ],
        scratch_shapes=[pltpu.VMEM((hp, HEAD_DIM, HEAD_DIM), F32),
                        pltpu.VMEM((chunk, chunk), BF16),
                        pltpu.VMEM((chunk, chunk), jnp.int32)],
        compiler_params=_params(2),
        name="hgrn_branch",
    )(*args)


def _merge_kernel(ha_ref, ob_ref, ga_ref, gb_ref, bg_ref, wa_ref, wb_ref, m_ref, wa16_ref, wb16_ref):
    @pl.when(pl.program_id(1) == 0)
    def _():
        wa16_ref[...] = wa_ref[...].astype(BF16)
        wb16_ref[...] = wb_ref[...].astype(BF16)

    ya = _dot(ha_ref[...], wa16_ref[...])
    yb = _dot(ob_ref[...], wb16_ref[...])
    ga = _sigmoid(ga_ref[...].astype(F32) + bg_ref[0:1, :])
    gb = _sigmoid(gb_ref[...].astype(F32) + bg_ref[1:2, :])
    m_ref[...] = (ga * ya + gb * yb).astype(m_ref.dtype)


def _merge(h_a, ob, proj, b_gate, w_a, w_b, *, row0, tm, tn, gate_col0):
    n, d_a = h_a.shape
    d_b = ob.shape[1]
    d = w_a.shape[1]
    ga0 = gate_col0 // tn
    gb0 = (gate_col0 + d) // tn
    i0 = row0 // tm
    return pl.pallas_call(
        _merge_kernel,
        grid=(d // tn, n // tm),
        in_specs=[
            pl.BlockSpec((tm, d_a), lambda j, i: (i, 0)),
            pl.BlockSpec((tm, d_b), lambda j, i: (i, 0)),
            pl.BlockSpec((tm, tn), lambda j, i: (i0 + i, ga0 + j)),
            pl.BlockSpec((tm, tn), lambda j, i: (i0 + i, gb0 + j)),
            pl.BlockSpec((2, tn), lambda j, i: (0, j)),
            pl.BlockSpec((d_a, tn), lambda j, i: (0, j)),
            pl.BlockSpec((d_b, tn), lambda j, i: (0, j)),
        ],
        out_specs=pl.BlockSpec((tm, tn), lambda j, i: (i, j)),
        out_shape=jax.ShapeDtypeStruct((n, d), BF16),
        scratch_shapes=[pltpu.VMEM((d_a, tn), BF16), pltpu.VMEM((d_b, tn), BF16)],
        compiler_params=_params(2),
        name="merge",
    )(h_a, ob, proj, proj, b_gate, w_a, w_b)


def _outproj_kernel(m_ref, x_ref, nw_ref, w_ref, y_ref, *, tn):
    j = pl.program_id(1)
    cols = pl.ds(pl.multiple_of(j * tn, tn), tn)
    y_ref[:, cols] = _dot(m_ref[...], w_ref[...])

    @pl.when(j == pl.num_programs(1) - 1)
    def _():
        out = y_ref[...]
        ms = jnp.mean(out * out, axis=-1, keepdims=True)
        y_ref[...] = x_ref[...] + out * lax.rsqrt(ms + RMS_EPS) * nw_ref[...]


def _outproj(m, x2d, norm_w, w, *, tm, tn):
    n, d = x2d.shape
    return pl.pallas_call(
        functools.partial(_outproj_kernel, tn=tn),
        grid=(n // tm, d // tn),
        in_specs=[
            pl.BlockSpec((tm, d), lambda i, j: (i, 0)),
            pl.BlockSpec((tm, d), lambda i, j: (i, 0)),
            pl.BlockSpec((1, d), lambda i, j: (0, 0)),
            pl.BlockSpec((d, tn), lambda i, j: (0, j)),
        ],
        out_specs=pl.BlockSpec((tm, d), lambda i, j: (i, 0)),
        out_shape=jax.ShapeDtypeStruct((n, d), F32),
        compiler_params=_params(2),
        name="outproj",
    )(m, x2d, norm_w, w)


def _mix(x, proj, row0, pool_buf, state, start_pos, p, layer, *, hgrn_hp, hgrn_chunk, pool_tt, tm):
    n_streams, seq, d = x.shape
    d_a = p["w_pool"].shape[0] * p["w_pool"].shape[1]
    hk = p["lb_logits"].shape[1]
    buf = jnp.pad(pool_buf.astype(F32), ((0, 0), (POOL_HALO - POOL_BUF, 0), (0, 0)))
    h_a, new_buf = _pool_branch(proj, buf, p["w_pool"], p["pool_scale"], row0=row0,
                                n_streams=n_streams, seq=seq, tt=pool_tt, start_pos=start_pos)
    ob, s_fin = _hgrn_branch(proj, p["lb_logits"], p["g_norm"], state, row0=row0,
                             n_streams=n_streams, seq=seq, hp=hgrn_hp, chunk=hgrn_chunk, col0=2 * d_a,
                             layer=layer)
    m = _merge(h_a, ob, proj, p["b_gate"], p["w_branch_a"], p["w_branch_b"], row0=row0,
               tm=tm, tn=512, gate_col0=2 * d_a + 4 * hk)
    y = _outproj(m, x.reshape(n_streams * seq, d), p["norm_post"], p["w_out"], tm=tm, tn=512)
    return y.reshape(x.shape), new_buf, s_fin


def kernel(x_prompt, x_sample, state_pool, state_hgrn, norm_pre, norm_post, w_in, w_pool, pool_scale,
           lb_logits, g_norm, w_branch_a, w_branch_b, b_gate, w_out):
    depth = w_in.shape[0]
    past_len = 1024
    hp, hs = x_prompt, x_sample
    pool_p, hgrn_p, pool_s, hgrn_s = [], [], [], []
    for l in range(depth):
        p = {
            "norm_pre": norm_pre[l][None, :],
            "norm_post": norm_post[l][None, :],
            "w_in": w_in[l],
            "w_pool": w_pool[l].astype(BF16),
            "pool_scale": pool_scale[l][None, :],
            "lb_logits": lb_logits,
            "g_norm": g_norm[l][None, :],
            "w_branch_a": w_branch_a[l],
            "w_branch_b": w_branch_b[l],
            "b_gate": b_gate[l],
            "w_out": w_out[l].astype(BF16),
        }
        buf0 = jnp.zeros((hp.shape[0], POOL_BUF, state_pool.shape[-1]), F32)
        d = hp.shape[-1]
        n_p = hp.shape[0] * hp.shape[1]
        xn = _prenorm(hp.reshape(n_p, d), hs.reshape(-1, d), p["norm_pre"], tm=256)
        proj = _inproj(xn, p["w_in"], tm=512, tn=1024)
        hp, bp, sp = _mix(hp, proj, 0, buf0, None, 0, p, l,
                          hgrn_hp=8, hgrn_chunk=128, pool_tt=512, tm=512)
        hs, bs, ss = _mix(hs, proj, n_p, state_pool[l], state_hgrn[l], past_len, p, l,
                          hgrn_hp=32, hgrn_chunk=32, pool_tt=32, tm=512)
        pool_p.append(bp)
        hgrn_p.append(sp)
        pool_s.append(bs)
        hgrn_s.append(ss)
    return (hp, hs, jnp.stack(pool_p), jnp.stack(hgrn_p), jnp.stack(pool_s), jnp.stack(hgrn_s))
```
